```python
import math
import jax, jax.numpy as jnp
from jax import lax
import numpy as np

D_MODEL = 1024
BATCH = 8
SEQ = 4096
DEPTH = 2

D_CONV = D_MODEL // 2
CONV_WIDTH = 31
N_RNN_HEADS = 4
RNN_HEAD_DIM = 128
D_RNN = N_RNN_HEADS * RNN_HEAD_DIM
RNN_CHUNK = 64
N_ATT_HEADS = 8
D_LATENT = 128
ATT_HEAD_DIM = 64
D_ATT = N_ATT_HEADS * ATT_HEAD_DIM
N_IDX_HEADS = 4
IDX_HEAD_DIM = 64
TOPK_MAX = 256
Q_BLOCK = 128
N_BUCKETS = 32
MAX_EXACT = N_BUCKETS // 2
MAX_DISTANCE = 128
N_EXPERTS = 32
TOPK_EXPERTS = 4
D_EXPERT = 256
D_SHARED = 256
ROUTE_SCALE = 2.5
MOE_BLOCK = 128
N_BRANCH = 3
DEEPNORM_ALPHA = (2 * DEPTH) ** 0.25
DEEPNORM_BETA = (8 * DEPTH) ** -0.25
LN_EPS = 1e-5
SPLITS = (2 * D_CONV,
          D_RNN, D_RNN, D_RNN, D_RNN,
          N_ATT_HEADS * D_LATENT, D_LATENT,
          N_IDX_HEADS * IDX_HEAD_DIM, IDX_HEAD_DIM, N_IDX_HEADS,
          N_BRANCH * D_MODEL)
N_IN = sum(SPLITS)

kernel_name = "hybrid_conv_hgrn2_dsa_moe_deepnorm"


def layer_norm(x, g, b):
    xf = x.astype(jnp.float32)
    mu = jnp.mean(xf, axis=-1, keepdims=True)
    var = jnp.mean(jnp.square(xf - mu), axis=-1, keepdims=True)
    return ((xf - mu) * lax.rsqrt(var + LN_EPS) * g.astype(jnp.float32) + b.astype(jnp.float32)).astype(x.dtype)


def rms_norm(x, g):
    xf = x.astype(jnp.float32)
    return xf * lax.rsqrt(jnp.mean(jnp.square(xf), axis=-1, keepdims=True) + LN_EPS) * g.astype(jnp.float32)


def t5_bucket(n):
    nf = jnp.maximum(n, 1).astype(jnp.float32)
    large = MAX_EXACT + (jnp.log(nf / MAX_EXACT) / math.log(MAX_DISTANCE / MAX_EXACT)
                         * (N_BUCKETS - MAX_EXACT)).astype(jnp.int32)
    large = jnp.minimum(large, N_BUCKETS - 1)
    return jnp.where(n < MAX_EXACT, n, large)


def conformer_conv(glu_in, dw, db, ln_g, ln_b, w_proj):
    a, gate = jnp.split(glu_in, 2, axis=-1)
    u = a * jax.nn.sigmoid(gate)
    u = lax.conv_general_dilated(u, dw[:, None, :], window_strides=(1,),
                                 padding=[(CONV_WIDTH - 1, 0)],
                                 dimension_numbers=('NWC', 'WIO', 'NWC'),
                                 feature_group_count=D_CONV) + db
    u = jax.nn.silu(layer_norm(u, ln_g, ln_b))
    return u @ w_proj


def hgrn2(q, fz, i, g, lb, norm_g, w_proj, out_dtype):
    B, S, _ = q.shape
    H, Dh, C = N_RNN_HEADS, RNN_HEAD_DIM, RNN_CHUNK
    nC = S // C
    fzf = fz.astype(jnp.float32)
    log_f = jnp.logaddexp(jnp.log(lb), jnp.log1p(-lb) + jax.nn.log_sigmoid(fzf))
    k = (1.0 - lb) * jax.nn.sigmoid(-fzf)

    def chunks(t):
        return t.astype(jnp.float32).reshape(B, nC, C, H, Dh).transpose(1, 0, 3, 2, 4)

    causal = jnp.tril(jnp.ones((C, C), dtype=bool))

    def step(state, inp):
        qc, kc, vc, lfc = inp
        A = jnp.cumsum(lfc, axis=2)
        o_inter = jnp.einsum('bhtd,bhde->bhte', qc * jnp.exp(A), state)
        rel = A[:, :, :, None, :] - A[:, :, None, :, :]
        decay = jnp.exp(jnp.where(causal[:, :, None], rel, -jnp.inf))
        P = jnp.einsum('bhtd,bhsd,bhtsd->bhts', qc, kc, decay)
        o = o_inter + jnp.einsum('bhts,bhse->bhte', P, vc)
        A_last = A[:, :, -1:, :]
        new_state = (jnp.exp(A_last[:, :, 0, :])[..., None] * state
                     + jnp.einsum('bhsd,bhse->bhde', kc * jnp.exp(A_last - A), vc))
        return new_state, o

    s0 = jnp.zeros((B, H, Dh, Dh), jnp.float32)
    _, o = lax.scan(step, s0, (chunks(q), chunks(k), chunks(i), chunks(log_f)))
    o = o.transpose(1, 0, 3, 2, 4).reshape(B, S, H, Dh)
    o = rms_norm(o, norm_g.reshape(H, Dh)) * jax.nn.sigmoid(g.astype(jnp.float32)).reshape(B, S, H, Dh)
    return o.reshape(B, S, D_RNN).astype(out_dtype) @ w_proj


def dsa_attention(q, c_raw, qi, ki, wi, kv_g, w_uv, rel_bias, w_proj, out_dtype):
    B, S, _ = q.shape
    H, DL = N_ATT_HEADS, D_LATENT
    K = min(TOPK_MAX, S // 4)
    nb = S // Q_BLOCK
    scale = DL ** -0.5
    q = q.astype(jnp.float32).reshape(B, S, H, DL)
    c = rms_norm(c_raw, kv_g)
    qi = qi.astype(jnp.float32).reshape(B, S, N_IDX_HEADS, IDX_HEAD_DIM)
    ki = ki.astype(jnp.float32)
    wi = wi.astype(jnp.float32)
    s_pos = jnp.arange(S)

    def blocks(t):
        return t.reshape(B, nb, Q_BLOCK, *t.shape[2:]).swapaxes(0, 1)

    def attend_block(inp):
        qb, qib, wb, t0 = inp
        t_pos = t0 + jnp.arange(Q_BLOCK)
        score = jnp.einsum('bth,bths->bts', wb, jax.nn.relu(jnp.einsum('bthd,bsd->bths', qib, ki)))
        score = jnp.where(s_pos[None, None, :] <= t_pos[None, :, None], score, -jnp.inf)
        _, sel = lax.top_k(score, K)
        c_sel = jax.vmap(lambda cb, ib: cb[ib])(c, sel)
        dist = t_pos[None, :, None] - sel
        bias = rel_bias[t5_bucket(jnp.maximum(dist, 0))]
        logits = jnp.einsum('bthc,btkc->bthk', qb, c_sel) * scale + bias.transpose(0, 1, 3, 2)
        logits = jnp.where((dist >= 0)[:, :, None, :], logits, -jnp.inf)
        p = jax.nn.softmax(logits, axis=-1)
        return jnp.einsum('bthk,btkc->bthc', p, c_sel)

    o = lax.map(attend_block, (blocks(q), blocks(qi), blocks(wi), jnp.arange(nb) * Q_BLOCK))
    o = o.swapaxes(0, 1).reshape(B, S, H, DL)
    o = jnp.einsum('bshc,hcv->bshv', o.astype(out_dtype), w_uv).reshape(B, S, D_ATT)
    return o @ w_proj


def moe(x, router_w, router_b, w_gu, w_dn, ws_gu, ws_dn):
    B, S, D = x.shape
    nb = S // MOE_BLOCK
    xc = x.reshape(B, nb, MOE_BLOCK, D).swapaxes(0, 1).reshape(nb, B * MOE_BLOCK, D)

    def expert_block(xb):
        scores = jax.nn.sigmoid((xb @ router_w).astype(jnp.float32))
        _, sel = lax.top_k(scores + router_b.astype(jnp.float32), TOPK_EXPERTS)
        w = jnp.take_along_axis(scores, sel, axis=-1)
        w = w / jnp.sum(w, axis=-1, keepdims=True) * ROUTE_SCALE
        gates = jnp.sum(jax.nn.one_hot(sel, N_EXPERTS, dtype=jnp.float32) * w[..., None], axis=1)
        a, u = jnp.split(jnp.einsum('td,edf->tef', xb, w_gu), 2, axis=-1)
        h = jax.nn.silu(a) * u * gates[..., None].astype(xb.dtype)
        routed = jnp.einsum('tef,efd->td', h, w_dn)
        a_s, u_s = jnp.split(xb @ ws_gu, 2, axis=-1)
        return routed + (jax.nn.silu(a_s) * u_s) @ ws_dn

    y = lax.map(expert_block, xc)
    return y.reshape(nb, B, MOE_BLOCK, D).swapaxes(0, 1).reshape(B, S, D)


def setup_inputs(seed: int = 0) -> dict:
    key = jax.random.key(seed)
    ks = jax.random.split(key, 32)
    f32 = jnp.float32

    def nrm(k, shape, scale):
        return jax.random.normal(k, shape, f32) * scale

    L, D, B = DEPTH, D_MODEL, DEEPNORM_BETA
    return {
        "x": nrm(ks[0], (BATCH, SEQ, D), 1.0),
        "ln_in_g": 1.0 + nrm(ks[1], (D,), 0.05),
        "ln_in_b": nrm(ks[2], (D,), 0.02),
        "w_in": nrm(ks[3], (L, D, N_IN), D ** -0.5),
        "b_in": nrm(ks[4], (L, N_IN), 0.02),
        "conv_dw": nrm(ks[5], (L, CONV_WIDTH, D_CONV), CONV_WIDTH ** -0.5),
        "conv_b": nrm(ks[6], (L, D_CONV), 0.02),
        "conv_ln_g": 1.0 + nrm(ks[7], (L, D_CONV), 0.05),
        "conv_ln_b": nrm(ks[8], (L, D_CONV), 0.02),
        "w_conv_proj": nrm(ks[9], (L, D_CONV, D), B * D_CONV ** -0.5),
        "hgrn_gamma": nrm(ks[10], (L, D_RNN), 0.5),
        "hgrn_norm_g": 1.0 + nrm(ks[11], (L, D_RNN), 0.05),
        "w_rnn_proj": nrm(ks[12], (L, D_RNN, D), B * D_RNN ** -0.5),
        "kv_norm_g": 1.0 + nrm(ks[13], (L, D_LATENT), 0.05),
        "w_uv": nrm(ks[14], (L, N_ATT_HEADS, D_LATENT, ATT_HEAD_DIM), B * D_LATENT ** -0.5),
        "w_att_proj": nrm(ks[15], (L, D_ATT, D), B * D_ATT ** -0.5),
        "rel_bias": nrm(ks[16], (N_BUCKETS, N_ATT_HEADS), 0.5),
        "w_out": nrm(ks[17], (L, D, D), B * D ** -0.5),
        "ln_mix_g": 1.0 + nrm(ks[18], (L, D), 0.05),
        "ln_mix_b": nrm(ks[19], (L, D), 0.02),
        "router_w": nrm(ks[20], (L, D, N_EXPERTS), D ** -0.5),
        "router_b": nrm(ks[21], (L, N_EXPERTS), 0.01),
        "w_expert_gu": nrm(ks[22], (L, N_EXPERTS, D, 2 * D_EXPERT), D ** -0.5),
        "w_expert_dn": nrm(ks[23], (L, N_EXPERTS, D_EXPERT, D), B * D_EXPERT ** -0.5),
        "w_shared_gu": nrm(ks[24], (L, D, 2 * D_SHARED), D ** -0.5),
        "w_shared_dn": nrm(ks[25], (L, D_SHARED, D), B * D_SHARED ** -0.5),
        "ln_ffn_g": 1.0 + nrm(ks[26], (L, D), 0.05),
        "ln_ffn_b": nrm(ks[27], (L, D), 0.02),
    }


def reference(x, ln_in_g, ln_in_b, w_in, b_in, conv_dw, conv_b, conv_ln_g, conv_ln_b, w_conv_proj,
              hgrn_gamma, hgrn_norm_g, w_rnn_proj, kv_norm_g, w_uv, w_att_proj, rel_bias, w_out,
              ln_mix_g, ln_mix_b, router_w, router_b, w_expert_gu, w_expert_dn, w_shared_gu,
              w_shared_dn, ln_ffn_g, ln_ffn_b):
    Bsz, S, D = x.shape
    split_points = [int(p) for p in np.cumsum(SPLITS)[:-1]]
    lb_all = jnp.cumsum(jax.nn.softmax(hgrn_gamma.astype(jnp.float32), axis=0), axis=0)
    lb_all = lb_all - lb_all[0]

    h = layer_norm(x, ln_in_g, ln_in_b)
    for l in range(DEPTH):
        proj = h @ w_in[l] + b_in[l]
        (conv_in, r_q, r_f, r_i, r_g, a_q, a_c, i_q, i_k, i_w, gate_in) = jnp.split(proj, split_points, axis=-1)
        y_conv = conformer_conv(conv_in, conv_dw[l], conv_b[l], conv_ln_g[l], conv_ln_b[l], w_conv_proj[l])
        y_rnn = hgrn2(r_q, r_f, r_i, r_g, lb_all[l], hgrn_norm_g[l], w_rnn_proj[l], h.dtype)
        y_att = dsa_attention(a_q, a_c, i_q, i_k, i_w, kv_norm_g[l], w_uv[l], rel_bias,
                              w_att_proj[l], h.dtype)
        g = jax.nn.sigmoid(gate_in).reshape(Bsz, S, N_BRANCH, D)
        merged = g[:, :, 0] * y_conv + g[:, :, 1] * y_rnn + g[:, :, 2] * y_att
        h = layer_norm(DEEPNORM_ALPHA * h + merged @ w_out[l], ln_mix_g[l], ln_mix_b[l])
        y_ffn = moe(h, router_w[l], router_b[l], w_expert_gu[l], w_expert_dn[l],
                    w_shared_gu[l], w_shared_dn[l])
        h = layer_norm(DEEPNORM_ALPHA * h + y_ffn, ln_ffn_g[l], ln_ffn_b[l])
    return h
```

```python
import functools
import math

import jax
import jax.numpy as jnp
from jax import lax
from jax.experimental import pallas as pl
from jax.experimental.pallas import tpu as pltpu

N_RNN_HEADS = 4
N_IDX_HEADS = 4
IDX_HEAD_DIM = 64
TOPK_MAX = 256
N_BUCKETS = 32
MAX_EXACT = N_BUCKETS // 2
MAX_DISTANCE = 128
TOPK_EXPERTS = 4
ROUTE_SCALE = 2.5
LN_EPS = 1e-5

LANES = 128
VMEM_LIMIT_BYTES = 56 * 1024 * 1024

Q_BLOCK = 128
RNN_CHUNK = 64
RNN_SUB = 16
NEG_BIG = -1e30
INT_MIN = -2147483648

BF16 = jnp.bfloat16
F32 = jnp.float32


def _cparams(*sem):
    return pltpu.CompilerParams(dimension_semantics=sem, vmem_limit_bytes=VMEM_LIMIT_BYTES)


def _const_spec(shape):
    nd = len(shape)
    return pl.BlockSpec(shape, lambda *_: (0,) * nd, pipeline_mode=pl.Buffered(1))


def _ln_rows(x, g, b):
    mu = jnp.mean(x, axis=-1, keepdims=True)
    xc = x - mu
    var = jnp.mean(xc * xc, axis=-1, keepdims=True)
    return xc * lax.rsqrt(var + LN_EPS) * g + b


def _sigmoid(x):
    return 1.0 / (1.0 + jnp.exp(-x))


def _dot(a, b):
    return jnp.dot(a, b, preferred_element_type=F32)


def _dot_nt(a, b):
    return lax.dot_general(a, b, (((1,), (1,)), ((), ())), preferred_element_type=F32)


def _dot_tn(a, b):
    return lax.dot_general(a, b, (((0,), (0,)), ((), ())), preferred_element_type=F32)


def _ln_kernel(x_ref, g_ref, b_ref, o_ref):
    o_ref[...] = _ln_rows(x_ref[...], g_ref[...], b_ref[...])


def _layer_norm(x, g, b, tm):
    T, D = x.shape
    return pl.pallas_call(
        _ln_kernel,
        grid=(T // tm,),
        in_specs=[pl.BlockSpec((tm, D), lambda i: (i, 0)), _const_spec((1, D)), _const_spec((1, D))],
        out_specs=pl.BlockSpec((tm, D), lambda i: (i, 0)),
        out_shape=jax.ShapeDtypeStruct((T, D), F32),
        compiler_params=_cparams("parallel"),
        name="ln_in",
    )(x, g.reshape(1, D), b.reshape(1, D))


def _proj_kernel(dims, h_ref, w_ref, b_ref, kvg_ref,
                 u_ref, rq_ref, rf_ref, ri_ref, rg_ref, aq_ref, c_ref, iq_ref, ik_ref, iw_ref, gt_ref):
    dc, dr, daq, dl, diq, dg = dims
    x = h_ref[...].astype(BF16)
    off = [0]

    def seg(n):
        o = off[0]
        off[0] = o + n
        return _dot(x, w_ref[:, o:o + n]) + b_ref[:, o:o + n]

    a = seg(dc)
    gate = seg(dc)
    u_ref[...] = (a * _sigmoid(gate)).astype(BF16)
    rq_ref[...] = seg(dr).astype(BF16)
    rf_ref[...] = seg(dr)
    ri_ref[...] = seg(dr).astype(BF16)
    rg_ref[...] = _sigmoid(seg(dr)).astype(BF16)
    scale = dl ** -0.5
    for j in range(daq // 512):
        aq_ref[:, j * 512:(j + 1) * 512] = (seg(512) * scale).astype(BF16)
    c = seg(dl)
    c = c * lax.rsqrt(jnp.mean(c * c, axis=-1, keepdims=True) + LN_EPS) * kvg_ref[...]
    c_ref[...] = c.astype(BF16)
    iq_ref[...] = seg(diq).astype(BF16)
    ik_ref[...] = seg(LANES).astype(BF16)
    iw_ref[...] = seg(LANES)
    for j in range(dg // 512):
        gt_ref[:, j * 512:(j + 1) * 512] = _sigmoid(seg(512)).astype(BF16)


def _in_projection(h, w, b, kv_g, dims, tm):
    T, D = h.shape
    dc, dr, daq, dl, diq, dg = dims
    n_pad = w.shape[1]
    widths = [(dc, BF16), (dr, BF16), (dr, F32), (dr, BF16), (dr, BF16), (daq, BF16), (dl, BF16),
              (diq, BF16), (LANES, BF16), (LANES, F32), (dg, BF16)]
    return pl.pallas_call(
        functools.partial(_proj_kernel, dims),
        grid=(T // tm,),
        in_specs=[pl.BlockSpec((tm, D), lambda i: (i, 0)), _const_spec((D, n_pad)),
                  _const_spec((1, n_pad)), _const_spec((1, dl))],
        out_specs=[pl.BlockSpec((tm, n), lambda i: (i, 0)) for n, _ in widths],
        out_shape=[jax.ShapeDtypeStruct((T, n), dt) for n, dt in widths],
        compiler_params=_cparams("parallel"),
        name="in_proj",
    )(h, w, b, kv_g.reshape(1, dl))


def _conv_kernel(width, ts, sub, u_ref, halo_ref, dw_ref, db_ref, g_ref, b_ref, o_ref, buf_ref):
    i = pl.program_id(1)
    halo = halo_ref[0].astype(F32)
    buf_ref[0:32, :] = jnp.where(i > 0, halo, 0.0)
    buf_ref[32:32 + ts, :] = u_ref[0].astype(F32)
    base = 32 - (width - 1)
    for s in range(ts // sub):
        acc = jnp.zeros((sub, u_ref.shape[2]), F32) + db_ref[...]
        for j in range(width):
            r0 = base + s * sub + j
            acc = acc + buf_ref[r0:r0 + sub, :] * dw_ref[j:j + 1, :]
        y = _ln_rows(acc, g_ref[...], b_ref[...])
        o_ref[0, s * sub:(s + 1) * sub, :] = (y * _sigmoid(y)).astype(BF16)


def _conformer_conv(u, dw, db, g, b, ts):
    B, S, C = u.shape
    width = dw.shape[0]
    assert width - 1 <= 32
    sub = min(64, ts)
    hb = ts // 32
    return pl.pallas_call(
        functools.partial(_conv_kernel, width, ts, sub),
        grid=(B, S // ts),
        in_specs=[pl.BlockSpec((1, ts, C), lambda bi, i: (bi, i, 0)),
                  pl.BlockSpec((1, 32, C), lambda bi, i: (bi, jnp.maximum(i * hb - 1, 0), 0)),
                  _const_spec((width, C)), _const_spec((1, C)), _const_spec((1, C)), _const_spec((1, C))],
        out_specs=pl.BlockSpec((1, ts, C), lambda bi, i: (bi, i, 0)),
        out_shape=jax.ShapeDtypeStruct((B, S, C), BF16),
        scratch_shapes=[pltpu.VMEM((ts + 32, C), F32)],
        compiler_params=_cparams("parallel", "arbitrary"),
        name="conv",
    )(u, u, dw, db.reshape(1, C), g.reshape(1, C), b.reshape(1, C))


def _hgrn_chunk(q, z, v, lb, st):
    C, Dh = q.shape
    ls = jnp.minimum(z, 0.0) - jnp.log1p(jnp.exp(-jnp.abs(z)))
    a_ = jnp.log(lb)
    b_ = jnp.log1p(-lb) + ls
    lf = jnp.maximum(a_, b_) + jnp.log1p(jnp.exp(-jnp.abs(a_ - b_)))
    kk = (1.0 - lb) * _sigmoid(-z)

    row = lax.broadcasted_iota(jnp.int32, (C, C), 0)
    col = lax.broadcasted_iota(jnp.int32, (C, C), 1)
    tri = (col <= row).astype(F32)
    A = jnp.dot(tri, lf, preferred_element_type=F32, precision=lax.Precision.HIGHEST)

    o = _dot_nt((q * jnp.exp(A)).astype(BF16), st.astype(BF16))

    rmod = lax.broadcasted_iota(jnp.int32, (C, Dh), 0) % RNN_SUB
    parts = []
    for d in range(RNN_SUB):
        if d == 0:
            e = q * kk
        else:
            a_s = pltpu.roll(A, d, axis=0)
            k_s = pltpu.roll(kk, d, axis=0)
            e = jnp.exp(jnp.where(rmod >= d, A - a_s, NEG_BIG)) * q * k_s
        parts.append(e.astype(BF16))
    ones = jnp.ones((Dh, LANES), BF16)
    rs = _dot(jnp.concatenate(parts, axis=0), ones)
    P = jnp.zeros((C, C), F32)
    for d in range(RNN_SUB):
        P = P + jnp.where(col == row - d, rs[d * C:(d + 1) * C, :C], 0.0)

    rowd = lax.broadcasted_iota(jnp.int32, (C, Dh), 0)
    blocks = [jnp.zeros((RNN_SUB, C), F32)]
    for i in range(1, C // RNN_SUB):
        r0 = i * RNN_SUB
        a_i = A[r0 - 1:r0, :]
        qi = q[r0:r0 + RNN_SUB] * jnp.exp(A[r0:r0 + RNN_SUB] - a_i)
        ki = jnp.where(rowd < r0, kk * jnp.exp(jnp.minimum(a_i - A, 0.0)), 0.0)
        blocks.append(_dot_nt(qi.astype(BF16), ki.astype(BF16)))
    P = P + jnp.concatenate(blocks, axis=0)
    o = o + _dot(P.astype(BF16), v.astype(BF16))

    a_last = A[C - 1:C, :]
    kd = kk * jnp.exp(a_last - A)
    st_new = st * jnp.exp(a_last) + _dot_tn(v.astype(BF16), kd.astype(BF16))
    return o, st_new


def _hgrn_kernel(n_chunks, q_ref, z_ref, v_ref, g_ref, lb_ref, ng_ref, o_ref, st_ref):
    @pl.when(pl.program_id(2) == 0)
    def _():
        st_ref[...] = jnp.zeros_like(st_ref)

    lb = lb_ref[...]
    st = st_ref[...]
    C = RNN_CHUNK
    for c in range(n_chunks):
        sl = slice(c * C, (c + 1) * C)
        o, st = _hgrn_chunk(q_ref[0, sl, :].astype(F32), z_ref[0, sl, :], v_ref[0, sl, :].astype(F32), lb, st)
        o = o * lax.rsqrt(jnp.mean(o * o, axis=-1, keepdims=True) + LN_EPS) * ng_ref[...]
        o_ref[0, sl, :] = (o * g_ref[0, sl, :].astype(F32)).astype(BF16)
    st_ref[...] = st


def _hgrn2(rq, rf, ri, rg, lb, norm_g, rows):
    B, S, DR = rq.shape
    H = N_RNN_HEADS
    Dh = DR // H
    blk = pl.BlockSpec((1, rows, Dh), lambda b, h, i: (b, i, h))
    vec = pl.BlockSpec((1, Dh), lambda b, h, i: (0, h))
    return pl.pallas_call(
        functools.partial(_hgrn_kernel, rows // RNN_CHUNK),
        grid=(B, H, S // rows),
        in_specs=[blk, blk, blk, blk, vec, vec],
        out_specs=blk,
        out_shape=jax.ShapeDtypeStruct((B, S, DR), BF16),
        scratch_shapes=[pltpu.VMEM((Dh, Dh), F32)],
        compiler_params=_cparams("parallel", "parallel", "arbitrary"),
        name="hgrn2",
    )(rq, rf, ri, rg, lb.reshape(1, DR), norm_g.reshape(1, DR))


def _dsa_kernel(topk, n_heads, aq_ref, c_ref, iq_ref, ik_ref, iw_ref, bias_ref, wuv_ref, o_ref,
                key_ref, mask_ref, lg_ref):
    QB = Q_BLOCK
    qi = pl.program_id(1)
    nk = qi + 1
    row = lax.broadcasted_iota(jnp.int32, (QB, QB), 0)
    col = lax.broadcasted_iota(jnp.int32, (QB, QB), 1)

    iq = iq_ref[0]
    q_st = jnp.concatenate([iq[:, h * IDX_HEAD_DIM:(h + 1) * IDX_HEAD_DIM] for h in range(N_IDX_HEADS)], axis=0)
    iw = iw_ref[0]
    wb = [jnp.broadcast_to(iw[:, h:h + 1], (QB, QB)) for h in range(N_IDX_HEADS)]

    def score_body(j, carry):
        kj = ik_ref[0, pl.ds(pl.multiple_of(j * QB, QB), QB), :][:, :IDX_HEAD_DIM]
        s4 = _dot_nt(q_st, kj)
        sc = jnp.zeros((QB, QB), F32)
        for h in range(N_IDX_HEADS):
            sc = sc + wb[h] * jnp.maximum(s4[h * QB:(h + 1) * QB], 0.0)
        sc = sc + 0.0
        bits = pltpu.bitcast(sc, jnp.int32)
        key = bits ^ ((bits >> 31) & 0x7FFFFFFF)
        valid = (j < qi) | (col <= row)
        key_ref[j] = jnp.where(valid, key, INT_MIN)
        return carry

    lax.fori_loop(0, nk, score_body, 0)

    def bit_body(it, cur):
        bit = lax.shift_left(jnp.int32(1), 31 - it)
        cand = cur ^ bit
        cand_b = jnp.broadcast_to(cand, (QB, QB))

        def cnt_body(j, acc):
            return acc + jnp.where(key_ref[j] >= cand_b, 1, 0)

        acc = lax.fori_loop(0, nk, cnt_body, jnp.zeros((QB, QB), jnp.int32))
        cnt = jnp.sum(acc, axis=1, keepdims=True)
        return jnp.where(cnt >= topk, cand, cur)

    thr = lax.fori_loop(0, 32, bit_body, jnp.full((QB, 1), INT_MIN, jnp.int32))
    thr_b = jnp.broadcast_to(thr, (QB, QB))

    def gt_body(j, acc):
        return acc + jnp.where(key_ref[j] > thr_b, 1, 0)

    n_gt = jnp.sum(lax.fori_loop(0, nk, gt_body, jnp.zeros((QB, QB), jnp.int32)), axis=1, keepdims=True)
    need_b = jnp.broadcast_to((topk - n_gt).astype(F32), (QB, QB))

    cum_rhs = jnp.concatenate([(row <= col).astype(BF16), jnp.ones((QB, QB), BF16)], axis=1)

    def sel_body(j, carry):
        key = key_ref[j]
        eq = key == thr_b
        cs = _dot(jnp.where(eq, 1.0, 0.0).astype(BF16), cum_rhs)
        rank = carry + cs[:, :QB]
        sel = (key > thr_b) | (eq & (rank <= need_b))
        sel = sel & (key > INT_MIN)
        mask_ref[j] = jnp.where(sel, 0.0, NEG_BIG)
        return carry + cs[:, QB:]

    lax.fori_loop(0, nk, sel_body, jnp.zeros((QB, QB), F32))

    dl = c_ref.shape[2]
    out = jnp.zeros((QB, o_ref.shape[2]), F32)
    for h in range(n_heads):
        qh = aq_ref[0, :, h * dl:(h + 1) * dl]

        def ckeys(j):
            return c_ref[0, pl.ds(pl.multiple_of(j * QB, QB), QB), :]

        def far_body(j, m_acc):
            lg = _dot_nt(qh, ckeys(j)) + mask_ref[j]
            lg_ref[j] = lg
            return jnp.maximum(m_acc, lg)

        m_acc = lax.fori_loop(0, jnp.maximum(qi - 1, 0), far_body, jnp.full((QB, QB), NEG_BIG, F32))
        lg_ref[qi] = _dot_nt(qh, ckeys(qi)) + mask_ref[qi] + bias_ref[0, h]
        jp = jnp.maximum(qi - 1, 0)
        lg_prev = _dot_nt(qh, ckeys(jp)) + mask_ref[jp] + bias_ref[1, h]

        @pl.when(qi > 0)
        def _():
            lg_ref[jp] = lg_prev

        m_acc = jnp.maximum(m_acc, lg_ref[qi])
        m_acc = jnp.maximum(m_acc, jnp.where(qi > 0, lg_prev, NEG_BIG))
        m_b = jnp.broadcast_to(jnp.max(m_acc, axis=1, keepdims=True), (QB, QB))

        def pv_body(j, carry):
            l_acc, acc = carry
            p = jnp.exp(lg_ref[j] - m_b)
            return l_acc + p, acc + _dot(p.astype(BF16), ckeys(j))

        l_acc, acc = lax.fori_loop(0, nk, pv_body, (jnp.zeros((QB, QB), F32), jnp.zeros((QB, dl), F32)))
        oh = acc / jnp.sum(l_acc, axis=1, keepdims=True)
        out = out + _dot(oh.astype(BF16), wuv_ref[h])
    o_ref[0] = out.astype(BF16)


def _dsa_attention(aq, c, iq, ik, iw, bias_near, wuv_pad):
    B, S, DQ = aq.shape
    dl = c.shape[2]
    n_heads = DQ // dl
    d_att = wuv_pad.shape[2]
    topk = min(TOPK_MAX, S // 4)
    QB = Q_BLOCK
    nkb = S // QB
    return pl.pallas_call(
        functools.partial(_dsa_kernel, topk, n_heads),
        grid=(B, nkb),
        in_specs=[pl.BlockSpec((1, QB, DQ), lambda b, i: (b, i, 0)),
                  pl.BlockSpec((1, S, dl), lambda b, i: (b, 0, 0)),
                  pl.BlockSpec((1, QB, iq.shape[2]), lambda b, i: (b, i, 0)),
                  pl.BlockSpec((1, S, LANES), lambda b, i: (b, 0, 0)),
                  pl.BlockSpec((1, QB, LANES), lambda b, i: (b, i, 0)),
                  _const_spec(bias_near.shape), _const_spec(wuv_pad.shape)],
        out_specs=pl.BlockSpec((1, QB, d_att), lambda b, i: (b, i, 0)),
        out_shape=jax.ShapeDtypeStruct((B, S, d_att), BF16),
        scratch_shapes=[pltpu.VMEM((nkb, QB, QB), jnp.int32), pltpu.VMEM((nkb, QB, QB), F32),
                        pltpu.VMEM((nkb, QB, QB), F32)],
        compiler_params=_cparams("parallel", "arbitrary"),
        name="dsa",
    )(aq, c, iq, ik, iw, bias_near, wuv_pad)


def _t5_bucket(n):
    nf = jnp.maximum(n, 1).astype(F32)
    large = MAX_EXACT + (jnp.log(nf / MAX_EXACT) / math.log(MAX_DISTANCE / MAX_EXACT)
                         * (N_BUCKETS - MAX_EXACT)).astype(jnp.int32)
    large = jnp.minimum(large, N_BUCKETS - 1)
    return jnp.where(n < MAX_EXACT, n, large)


def _near_bias(rel_bias):
    QB = Q_BLOCK
    assert QB >= MAX_DISTANCE
    r = jnp.arange(QB)[:, None] - jnp.arange(QB)[None, :]
    far = rel_bias[N_BUCKETS - 1]
    diag = rel_bias[_t5_bucket(jnp.maximum(r, 0))] - far
    prev = rel_bias[_t5_bucket(r + QB)] - far
    return jnp.stack([diag, prev]).transpose(0, 3, 1, 2).astype(F32)


def _merge_kernel(alpha, D, h_ref, u_ref, r_ref, a_ref, gt_ref, wc_ref, wr_ref, wa_ref, wo_ref, g_ref, b_ref,
                  o_ref):
    y = gt_ref[:, 0:D].astype(F32) * _dot(u_ref[...], wc_ref[...])
    y = y + gt_ref[:, D:2 * D].astype(F32) * _dot(r_ref[...], wr_ref[...])
    y = y + gt_ref[:, 2 * D:3 * D].astype(F32) * _dot(a_ref[...], wa_ref[...])
    z = alpha * h_ref[...] + _dot(y.astype(BF16), wo_ref[...])
    o_ref[...] = _ln_rows(z, g_ref[...], b_ref[...])


def _merge(h, u, r, a, gt, wc, wr, wa, wo, g, b, alpha, tm):
    T, D = h.shape
    row = lambda n: pl.BlockSpec((tm, n), lambda i: (i, 0))
    return pl.pallas_call(
        functools.partial(_merge_kernel, alpha, D),
        grid=(T // tm,),
        in_specs=[row(D), row(u.shape[1]), row(r.shape[1]), row(a.shape[1]), row(3 * D),
                  _const_spec(wc.shape), _const_spec(wr.shape), _const_spec(wa.shape), _const_spec(wo.shape),
                  _const_spec((1, D)), _const_spec((1, D))],
        out_specs=row(D),
        out_shape=jax.ShapeDtypeStruct((T, D), F32),
        compiler_params=_cparams("parallel"),
        name="merge",
    )(h, u, r, a, gt, wc, wr, wa, wo, g.reshape(1, D), b.reshape(1, D))


def _split_bf16(x):
    hi = x.astype(BF16)
    return hi, (x - hi.astype(F32)).astype(BF16)


def _moe_kernel(alpha, n_exp, de, h_ref, rw_hi_ref, rw_lo_ref, rb_ref, wgu_ref, wdn_ref, wsgu_ref, wsdn_ref,
                g_ref, b_ref, o_ref, xb_ref, gate_ref, acc_ref):
    e = pl.program_id(1)
    lane = lax.broadcasted_iota(jnp.int32, gate_ref.shape, 1)

    @pl.when(e == 0)
    def _():
        x = h_ref[...]
        x_hi, x_lo = _split_bf16(x)
        xb_ref[...] = x_hi
        logits = _dot(x_hi, rw_hi_ref[...]) + (_dot(x_lo, rw_hi_ref[...]) + _dot(x_hi, rw_lo_ref[...]))
        scores = _sigmoid(logits)
        sb = scores + rb_ref[...]
        sel = jnp.zeros(sb.shape, jnp.bool_)
        for _ in range(TOPK_EXPERTS):
            m = jnp.max(sb, axis=-1, keepdims=True)
            first = jnp.min(jnp.where(sb == m, lane, n_exp), axis=-1, keepdims=True)
            hit = lane == first
            sel = sel | hit
            sb = jnp.where(hit, -jnp.inf, sb)
        w = jnp.where(sel, scores, 0.0)
        gate_ref[...] = w / jnp.sum(w, axis=-1, keepdims=True) * ROUTE_SCALE
        ds = wsdn_ref.shape[0]
        s = _dot(x_hi, wsgu_ref[...])
        a_s, u_s = s[:, :ds], s[:, ds:]
        acc_ref[...] = _dot((a_s * _sigmoid(a_s) * u_s).astype(BF16), wsdn_ref[...])

    xb = xb_ref[...]
    au = _dot(xb, wgu_ref[0])
    a, u = au[:, :de], au[:, de:]
    ge = jnp.sum(jnp.where(lane == e, gate_ref[...], 0.0), axis=-1, keepdims=True)
    hmid = a * _sigmoid(a) * u * ge
    acc_ref[...] += _dot(hmid.astype(BF16), wdn_ref[0])

    @pl.when(e == n_exp - 1)
    def _():
        o_ref[...] = _ln_rows(alpha * h_ref[...] + acc_ref[...], g_ref[...], b_ref[...])


def _moe(h, rw, rb, wgu, wdn, wsgu, wsdn, g, b, alpha, tm):
    T, D = h.shape
    n_exp, _, de2 = wgu.shape
    de = de2 // 2
    rw_hi, rw_lo = _split_bf16(rw)
    return pl.pallas_call(
        functools.partial(_moe_kernel, alpha, n_exp, de),
        grid=(T // tm, n_exp),
        in_specs=[pl.BlockSpec((tm, D), lambda i, e: (i, 0)),
                  _const_spec(rw.shape), _const_spec(rw.shape), _const_spec((1, n_exp)),
                  pl.BlockSpec((1, D, de2), lambda i, e: (e, 0, 0)),
                  pl.BlockSpec((1, de, D), lambda i, e: (e, 0, 0)),
                  _const_spec(wsgu.shape), _const_spec(wsdn.shape),
                  _const_spec((1, D)), _const_spec((1, D))],
        out_specs=pl.BlockSpec((tm, D), lambda i, e: (i, 0)),
        out_shape=jax.ShapeDtypeStruct((T, D), F32),
        scratch_shapes=[pltpu.VMEM((tm, D), BF16), pltpu.VMEM((tm, n_exp), F32), pltpu.VMEM((tm, D), F32)],
        compiler_params=_cparams("parallel", "arbitrary"),
        name="moe",
    )(h, rw_hi, rw_lo, rb.reshape(1, n_exp), wgu, wdn, wsgu, wsdn, g.reshape(1, D), b.reshape(1, D))


def _pack_in_weights(w, b, dims):
    dc, dr, daq, dl, diq, dg = dims
    sizes = [2 * dc, dr, dr, dr, dr, daq, dl, diq, IDX_HEAD_DIM, N_IDX_HEADS, dg]
    assert sum(sizes) == w.shape[1]
    offs = [0]
    for s in sizes:
        offs.append(offs[-1] + s)
    cols = [w[:, offs[i]:offs[i + 1]] for i in range(len(sizes))]
    bias = [b[offs[i]:offs[i + 1]] for i in range(len(sizes))]

    def pad(x, n):
        return jnp.pad(x, [(0, 0)] * (x.ndim - 1) + [(0, n - x.shape[-1])])

    cols[8], bias[8] = pad(cols[8], LANES), pad(bias[8], LANES)
    cols[9], bias[9] = pad(cols[9], LANES), pad(bias[9], LANES)
    return jnp.concatenate(cols, axis=1).astype(BF16), jnp.concatenate(bias).reshape(1, -1).astype(F32)


def kernel(x, ln_in_g, ln_in_b, w_in, b_in, conv_dw, conv_b, conv_ln_g, conv_ln_b, w_conv_proj, hgrn_gamma,
           hgrn_norm_g, w_rnn_proj, kv_norm_g, w_uv, w_att_proj, rel_bias, w_out, ln_mix_g, ln_mix_b,
           router_w, router_b, w_expert_gu, w_expert_dn, w_shared_gu, w_shared_dn, ln_ffn_g, ln_ffn_b):
    B, S, D = x.shape
    T = B * S
    depth = w_in.shape[0]
    dc = conv_dw.shape[2]
    dr = hgrn_gamma.shape[1]
    n_heads, dl, dv = w_uv.shape[1:]
    daq = n_heads * dl
    diq = N_IDX_HEADS * IDX_HEAD_DIM
    dims = (dc, dr, daq, dl, diq, 3 * D)
    alpha = (2 * depth) ** 0.25

    tm = min(512, T)
    lb_all = jnp.cumsum(jax.nn.softmax(hgrn_gamma.astype(F32), axis=0), axis=0)
    lb_all = lb_all - lb_all[0]
    bias_near = _near_bias(rel_bias)

    h = _layer_norm(x.reshape(T, D), ln_in_g, ln_in_b, tm)
    for l in range(depth):
        w_pack, b_pack = _pack_in_weights(w_in[l], b_in[l], dims)
        u, rq, rf, ri, rg, aq, c, iq, ik, iw, gt = _in_projection(h, w_pack, b_pack, kv_norm_g[l], dims, tm)

        def seq(t):
            return t.reshape(B, S, t.shape[-1])

        y_conv = _conformer_conv(seq(u), conv_dw[l], conv_b[l], conv_ln_g[l], conv_ln_b[l], min(512, S))
        y_rnn = _hgrn2(seq(rq), seq(rf), seq(ri), seq(rg), lb_all[l], hgrn_norm_g[l], min(256, S))
        wuv_pad = jnp.zeros((n_heads, dl, n_heads * dv), F32)
        for hh in range(n_heads):
            wuv_pad = wuv_pad.at[hh, :, hh * dv:(hh + 1) * dv].set(w_uv[l, hh])
        y_att = _dsa_attention(seq(aq), seq(c), seq(iq), seq(ik), seq(iw), bias_near, wuv_pad.astype(BF16))
        h = _merge(h, y_conv.reshape(T, -1), y_rnn.reshape(T, -1), y_att.reshape(T, -1), gt,
                   w_conv_proj[l].astype(BF16), w_rnn_proj[l].astype(BF16), w_att_proj[l].astype(BF16),
                   w_out[l].astype(BF16), ln_mix_g[l], ln_mix_b[l], alpha, tm)
        h = _moe(h, router_w[l], router_b[l], w_expert_gu[l].astype(BF16), w_expert_dn[l].astype(BF16),
                 w_shared_gu[l].astype(BF16), w_shared_dn[l].astype(BF16), ln_ffn_g[l], ln_ffn_b[l], alpha,
                 min(1024, T))
    return h.reshape(B, S, D)
```

```python
import functools
import math

import jax
import jax.numpy as jnp
from jax import lax
from jax.experimental import pallas as pl
from jax.experimental.pallas import tpu as pltpu

N_RNN_HEADS = 4
N_IDX_HEADS = 4
IDX_HEAD_DIM = 64
TOPK_MAX = 256
N_BUCKETS = 32
MAX_EXACT = N_BUCKETS // 2
MAX_DISTANCE = 128
TOPK_EXPERTS = 4
ROUTE_SCALE = 2.5
LN_EPS = 1e-5

LANES = 128
VMEM_LIMIT_BYTES = 56 * 1024 * 1024

Q_BLOCK = 128
KEY_CHUNK = 4
RNN_CHUNK = 64
RNN_SUB = 16
NEG_BIG = -1e30
INT_MIN = -2147483648
I16_MIN = -32768
LOG2E = math.log2(math.e)

BF16 = jnp.bfloat16
F32 = jnp.float32
I16 = jnp.int16
I32 = jnp.int32


def _cparams(*sem):
    return pltpu.CompilerParams(dimension_semantics=sem, vmem_limit_bytes=VMEM_LIMIT_BYTES)


def _const_spec(shape):
    nd = len(shape)
    return pl.BlockSpec(shape, lambda *_: (0,) * nd, pipeline_mode=pl.Buffered(1))


def _ln_rows(x, g, b):
    mu = jnp.mean(x, axis=-1, keepdims=True)
    xc = x - mu
    var = jnp.mean(xc * xc, axis=-1, keepdims=True)
    return xc * lax.rsqrt(var + LN_EPS) * g + b


def _sigmoid(x):
    return 1.0 / (1.0 + jnp.exp(-x))


def _dot(a, b):
    return jnp.dot(a, b, preferred_element_type=F32)


def _dot_nt(a, b):
    return lax.dot_general(a, b, (((1,), (1,)), ((), ())), preferred_element_type=F32)


def _dot_tn(a, b):
    return lax.dot_general(a, b, (((0,), (0,)), ((), ())), preferred_element_type=F32)


def _ln_kernel(x_ref, g_ref, b_ref, o_ref):
    o_ref[...] = _ln_rows(x_ref[...], g_ref[...], b_ref[...])


def _layer_norm(x, g, b, tm):
    T, D = x.shape
    return pl.pallas_call(
        _ln_kernel,
        grid=(T // tm,),
        in_specs=[pl.BlockSpec((tm, D), lambda i: (i, 0)), _const_spec((1, D)), _const_spec((1, D))],
        out_specs=pl.BlockSpec((tm, D), lambda i: (i, 0)),
        out_shape=jax.ShapeDtypeStruct((T, D), F32),
        compiler_params=_cparams("parallel"),
        name="ln_in",
    )(x, g.reshape(1, D), b.reshape(1, D))


def _proj_kernel(dims, h_ref, w_ref, b_ref, kvg_ref,
                 u_ref, rq_ref, rf_ref, ri_ref, rg_ref, aq_ref, c_ref, iq_ref, ik_ref, iw_ref, gt_ref):
    dc, dr, daq, dl, diq, dg = dims
    x = h_ref[...].astype(BF16)
    off = [0]

    def seg(n):
        o = off[0]
        off[0] = o + n
        return _dot(x, w_ref[:, o:o + n]) + b_ref[:, o:o + n]

    a = seg(dc)
    gate = seg(dc)
    u_ref[...] = (a * _sigmoid(gate)).astype(BF16)
    rq_ref[...] = seg(dr).astype(BF16)
    rf_ref[...] = seg(dr)
    ri_ref[...] = seg(dr).astype(BF16)
    rg_ref[...] = _sigmoid(seg(dr)).astype(BF16)
    scale = dl ** -0.5 * LOG2E
    for j in range(daq // 512):
        aq_ref[:, j * 512:(j + 1) * 512] = (seg(512) * scale).astype(BF16)
    c = seg(dl)
    c = c * lax.rsqrt(jnp.mean(c * c, axis=-1, keepdims=True) + LN_EPS) * kvg_ref[...]
    c_ref[...] = c.astype(BF16)
    iq_ref[...] = seg(diq).astype(BF16)
    ik_ref[...] = seg(LANES).astype(BF16)
    iw_ref[...] = seg(LANES)
    for j in range(dg // 512):
        gt_ref[:, j * 512:(j + 1) * 512] = _sigmoid(seg(512)).astype(BF16)


def _in_projection(h, w, b, kv_g, dims, tm):
    T, D = h.shape
    dc, dr, daq, dl, diq, dg = dims
    n_pad = w.shape[1]
    widths = [(dc, BF16), (dr, BF16), (dr, F32), (dr, BF16), (dr, BF16), (daq, BF16), (dl, BF16),
              (diq, BF16), (LANES, BF16), (LANES, F32), (dg, BF16)]
    return pl.pallas_call(
        functools.partial(_proj_kernel, dims),
        grid=(T // tm,),
        in_specs=[pl.BlockSpec((tm, D), lambda i: (i, 0)), _const_spec((D, n_pad)),
                  _const_spec((1, n_pad)), _const_spec((1, dl))],
        out_specs=[pl.BlockSpec((tm, n), lambda i: (i, 0)) for n, _ in widths],
        out_shape=[jax.ShapeDtypeStruct((T, n), dt) for n, dt in widths],
        compiler_params=_cparams("parallel"),
        name="in_proj",
    )(h, w, b, kv_g.reshape(1, dl))


def _conv_kernel(width, ts, sub, u_ref, halo_ref, dw_ref, db_ref, g_ref, b_ref, o_ref, buf_ref):
    i = pl.program_id(1)
    halo = halo_ref[0].astype(F32)
    buf_ref[0:32, :] = jnp.where(i > 0, halo, 0.0)
    buf_ref[32:32 + ts, :] = u_ref[0].astype(F32)
    base = 32 - (width - 1)
    for s in range(ts // sub):
        acc = jnp.zeros((sub, u_ref.shape[2]), F32) + db_ref[...]
        for j in range(width):
            r0 = base + s * sub + j
            acc = acc + buf_ref[r0:r0 + sub, :] * dw_ref[j:j + 1, :]
        y = _ln_rows(acc, g_ref[...], b_ref[...])
        o_ref[0, s * sub:(s + 1) * sub, :] = (y * _sigmoid(y)).astype(BF16)


def _conformer_conv(u, dw, db, g, b, ts):
    B, S, C = u.shape
    width = dw.shape[0]
    assert width - 1 <= 32
    sub = min(64, ts)
    hb = ts // 32
    return pl.pallas_call(
        functools.partial(_conv_kernel, width, ts, sub),
        grid=(B, S // ts),
        in_specs=[pl.BlockSpec((1, ts, C), lambda bi, i: (bi, i, 0)),
                  pl.BlockSpec((1, 32, C), lambda bi, i: (bi, jnp.maximum(i * hb - 1, 0), 0)),
                  _const_spec((width, C)), _const_spec((1, C)), _const_spec((1, C)), _const_spec((1, C))],
        out_specs=pl.BlockSpec((1, ts, C), lambda bi, i: (bi, i, 0)),
        out_shape=jax.ShapeDtypeStruct((B, S, C), BF16),
        scratch_shapes=[pltpu.VMEM((ts + 32, C), F32)],
        compiler_params=_cparams("parallel", "arbitrary"),
        name="conv",
    )(u, u, dw, db.reshape(1, C), g.reshape(1, C), b.reshape(1, C))


def _hgrn_chunk(q, z, v, lb, st):
    C, Dh = q.shape
    ls = jnp.minimum(z, 0.0) - jnp.log1p(jnp.exp(-jnp.abs(z)))
    a_ = jnp.log(lb)
    b_ = jnp.log1p(-lb) + ls
    lf = jnp.maximum(a_, b_) + jnp.log1p(jnp.exp(-jnp.abs(a_ - b_)))
    kk = (1.0 - lb) * _sigmoid(-z)

    row = lax.broadcasted_iota(I32, (C, C), 0)
    col = lax.broadcasted_iota(I32, (C, C), 1)
    tri = (col <= row).astype(F32)
    A = jnp.dot(tri, lf, preferred_element_type=F32, precision=lax.Precision.HIGHEST)

    o = _dot_nt((q * jnp.exp(A)).astype(BF16), st.astype(BF16))

    rmod = lax.broadcasted_iota(I32, (C, Dh), 0) % RNN_SUB
    parts = []
    for d in range(RNN_SUB):
        if d == 0:
            e = q * kk
        else:
            a_s = pltpu.roll(A, d, axis=0)
            k_s = pltpu.roll(kk, d, axis=0)
            e = jnp.exp(jnp.where(rmod >= d, A - a_s, NEG_BIG)) * q * k_s
        parts.append(e.astype(BF16))
    ones = jnp.ones((Dh, LANES), BF16)
    rs = _dot(jnp.concatenate(parts, axis=0), ones)
    P = jnp.zeros((C, C), F32)
    for d in range(RNN_SUB):
        P = P + jnp.where(col == row - d, rs[d * C:(d + 1) * C, :C], 0.0)

    rowd = lax.broadcasted_iota(I32, (C, Dh), 0)
    blocks = [jnp.zeros((RNN_SUB, C), F32)]
    for i in range(1, C // RNN_SUB):
        r0 = i * RNN_SUB
        a_i = A[r0 - 1:r0, :]
        qi = q[r0:r0 + RNN_SUB] * jnp.exp(A[r0:r0 + RNN_SUB] - a_i)
        ki = jnp.where(rowd < r0, kk * jnp.exp(jnp.minimum(a_i - A, 0.0)), 0.0)
        blocks.append(_dot_nt(qi.astype(BF16), ki.astype(BF16)))
    P = P + jnp.concatenate(blocks, axis=0)
    o = o + _dot(P.astype(BF16), v.astype(BF16))

    a_last = A[C - 1:C, :]
    kd = kk * jnp.exp(a_last - A)
    st_new = st * jnp.exp(a_last) + _dot_tn(v.astype(BF16), kd.astype(BF16))
    return o, st_new


def _hgrn_kernel(n_chunks, q_ref, z_ref, v_ref, g_ref, lb_ref, ng_ref, o_ref, st_ref):
    @pl.when(pl.program_id(2) == 0)
    def _():
        st_ref[...] = jnp.zeros_like(st_ref)

    lb = lb_ref[...]
    st = st_ref[...]
    C = RNN_CHUNK
    for c in range(n_chunks):
        sl = slice(c * C, (c + 1) * C)
        o, st = _hgrn_chunk(q_ref[0, sl, :].astype(F32), z_ref[0, sl, :], v_ref[0, sl, :].astype(F32), lb, st)
        o = o * lax.rsqrt(jnp.mean(o * o, axis=-1, keepdims=True) + LN_EPS) * ng_ref[...]
        o_ref[0, sl, :] = (o * g_ref[0, sl, :].astype(F32)).astype(BF16)
    st_ref[...] = st


def _hgrn2(rq, rf, ri, rg, lb, norm_g, rows):
    B, S, DR = rq.shape
    H = N_RNN_HEADS
    Dh = DR // H
    blk = pl.BlockSpec((1, rows, Dh), lambda b, h, i: (b, i, h))
    vec = pl.BlockSpec((1, Dh), lambda b, h, i: (0, h))
    return pl.pallas_call(
        functools.partial(_hgrn_kernel, rows // RNN_CHUNK),
        grid=(B, H, S // rows),
        in_specs=[blk, blk, blk, blk, vec, vec],
        out_specs=blk,
        out_shape=jax.ShapeDtypeStruct((B, S, DR), BF16),
        scratch_shapes=[pltpu.VMEM((Dh, Dh), F32)],
        compiler_params=_cparams("parallel", "parallel", "arbitrary"),
        name="hgrn2",
    )(rq, rf, ri, rg, lb.reshape(1, DR), norm_g.reshape(1, DR))


def _dsa_kernel(topk, n_heads, aq_ref, c_ref, iq_ref, ik_ref, iw_ref, bias_ref, wuv_ref, o_ref,
                cpad_ref, key_ref, hi_ref, lo_ref, mask_ref, m_ref, l_ref, acc_ref):
    QB, KC = Q_BLOCK, KEY_CHUNK
    PAD = KC - 1
    qi = pl.program_id(1)
    n_chunks = qi // KC + 1
    row = lax.broadcasted_iota(I32, (QB, QB), 0)
    col = lax.broadcasted_iota(I32, (QB, QB), 1)
    dl = c_ref.shape[2]

    @pl.when(qi == 0)
    def _():
        cpad_ref[0:PAD * QB, :] = jnp.zeros((PAD * QB, dl), BF16)
        cpad_ref[PAD * QB:, :] = c_ref[0]

    iq = iq_ref[0]
    q_st = jnp.concatenate([iq[:, h * IDX_HEAD_DIM:(h + 1) * IDX_HEAD_DIM] for h in range(N_IDX_HEADS)], axis=0)
    iw = iw_ref[0]
    wb = [jnp.broadcast_to(iw[:, h:h + 1], (QB, QB)) for h in range(N_IDX_HEADS)]
    dcr = col - row

    def score_body(ci, carry):
        kj = ik_ref[0, pl.ds(pl.multiple_of(ci * (KC * QB), KC * QB), KC * QB), :][:, :IDX_HEAD_DIM]
        s4 = _dot_nt(q_st, kj)
        for b in range(KC):
            j = ci * KC + b
            sc = jnp.zeros((QB, QB), F32)
            for h in range(N_IDX_HEADS):
                sc = sc + wb[h] * jnp.maximum(s4[h * QB:(h + 1) * QB, b * QB:(b + 1) * QB], 0.0)
            sc = sc + 0.0
            bits = pltpu.bitcast(sc, I32)
            key = bits ^ ((bits >> 31) & 0x7FFFFFFF)
            key = jnp.where(dcr <= (qi - j) * QB, key, INT_MIN)
            key_ref[j] = key
            hi_ref[j] = (key >> 16).astype(I16)
            lo_ref[j] = ((key & 0xFFFF) + I16_MIN).astype(I16)
        return carry

    lax.fori_loop(0, n_chunks, score_body, 0)

    def radix16(src_ref, need):
        def bit_body(it, cur):
            cand = cur + lax.shift_left(jnp.int32(1), 15 - it)
            cand16 = jnp.broadcast_to(cand, (QB, QB)).astype(I16)

            def cnt_body(ci, acc):
                for b in range(KC):
                    acc = acc + jnp.where(src_ref[ci * KC + b] >= cand16, jnp.int16(1), jnp.int16(0))
                return acc

            acc = lax.fori_loop(0, n_chunks, cnt_body, jnp.zeros((QB, QB), I16))
            cnt = jnp.sum(acc.astype(I32), axis=1, keepdims=True)
            return jnp.where(cnt >= need, cand, cur)

        return lax.fori_loop(0, 16, bit_body, jnp.full((QB, 1), I16_MIN, I32))

    t_hi = radix16(hi_ref, topk)
    t_hi16 = jnp.broadcast_to(t_hi, (QB, QB)).astype(I16)

    def mid_body(ci, acc):
        for b in range(KC):
            j = ci * KC + b
            hi = hi_ref[j]
            acc = acc + jnp.where(hi > t_hi16, jnp.int16(1), jnp.int16(0))
            lo_ref[j] = jnp.where(hi == t_hi16, lo_ref[j], jnp.int16(I16_MIN))
        return acc

    n_hi_gt = jnp.sum(lax.fori_loop(0, n_chunks, mid_body, jnp.zeros((QB, QB), I16)).astype(I32),
                      axis=1, keepdims=True)
    t_lo = radix16(lo_ref, topk - n_hi_gt)
    thr = t_hi * 65536 + (t_lo - I16_MIN)
    thr_b = jnp.broadcast_to(jnp.maximum(thr, INT_MIN + 1), (QB, QB))

    for p in range(PAD):
        mask_ref[p] = jnp.full((QB, QB), NEG_BIG, F32)

    def mask_body(ci, acc):
        for b in range(KC):
            j = ci * KC + b
            ge = key_ref[j] >= thr_b
            mask_ref[PAD + j] = jnp.where(ge, 0.0, NEG_BIG)
            acc = acc + jnp.where(ge, 1, 0)
        return acc

    n_sel = jnp.sum(lax.fori_loop(0, n_chunks, mask_body, jnp.zeros((QB, QB), I32)), axis=1, keepdims=True)

    @pl.when(jnp.max(n_sel) > topk)
    def _():
        def gt_body(j, acc):
            return acc + jnp.where(key_ref[j] > thr_b, 1, 0)

        n_gt = jnp.sum(lax.fori_loop(0, qi + 1, gt_body, jnp.zeros((QB, QB), I32)), axis=1, keepdims=True)
        need_b = jnp.broadcast_to((topk - n_gt).astype(F32), (QB, QB))
        cum_rhs = jnp.concatenate([(row <= col).astype(BF16), jnp.ones((QB, QB), BF16)], axis=1)

        def tie_body(j, carry):
            key = key_ref[j]
            eq = key == thr_b
            cs = _dot(jnp.where(eq, 1.0, 0.0).astype(BF16), cum_rhs)
            sel = (key > thr_b) | (eq & (carry + cs[:, :QB] <= need_b))
            mask_ref[PAD + j] = jnp.where(sel, 0.0, NEG_BIG)
            return carry + cs[:, QB:]

        lax.fori_loop(0, qi + 1, tie_body, jnp.zeros((QB, QB), F32))

    q_all = jnp.concatenate([aq_ref[0, :, h * dl:(h + 1) * dl] for h in range(n_heads)], axis=0)
    n_far = qi // KC

    def logits(start, near):
        cch = cpad_ref[pl.ds(pl.multiple_of(start * QB, QB), KC * QB), :]
        lg = _dot_nt(q_all, cch)
        out = []
        for h in range(n_heads):
            blks = []
            for b in range(KC):
                x = lg[h * QB:(h + 1) * QB, b * QB:(b + 1) * QB] + mask_ref[start + b]
                if near and b == KC - 1:
                    x = x + bias_ref[0, h]
                if near and b == KC - 2:
                    x = x + bias_ref[1, h]
                blks.append(x)
            out.append(blks)
        return cch, out

    def max_step(start, near):
        _, lg = logits(start, near)
        for h in range(n_heads):
            m = lg[h][0]
            for b in range(1, KC):
                m = jnp.maximum(m, lg[h][b])
            m_ref[h] = jnp.maximum(m_ref[h], m)

    def pv_step(start, near):
        cch, lg = logits(start, near)
        rows = []
        for h in range(n_heads):
            mb = m_ref[h]
            ps = [jnp.exp2(lg[h][b] - mb) for b in range(KC)]
            s = ps[0]
            for b in range(1, KC):
                s = s + ps[b]
            l_ref[h] += s
            rows.append(jnp.concatenate([p.astype(BF16) for p in ps], axis=1))
        acc_ref[...] += _dot(jnp.concatenate(rows, axis=0), cch)

    def far_loop(step):
        def body(m, carry):
            step(qi - KC * (m + 1), False)
            return carry
        lax.fori_loop(0, n_far, body, 0)

    m_ref[...] = jnp.full(m_ref.shape, NEG_BIG, F32)
    l_ref[...] = jnp.zeros(l_ref.shape, F32)
    acc_ref[...] = jnp.zeros(acc_ref.shape, F32)
    far_loop(max_step)
    max_step(qi, True)
    for h in range(n_heads):
        m_ref[h] = jnp.broadcast_to(jnp.max(m_ref[h], axis=1, keepdims=True), (QB, QB))
    far_loop(pv_step)
    pv_step(qi, True)

    out = jnp.zeros((QB, o_ref.shape[2]), F32)
    for h in range(n_heads):
        oh = acc_ref[h * QB:(h + 1) * QB, :] / jnp.sum(l_ref[h], axis=1, keepdims=True)
        out = out + _dot(oh.astype(BF16), wuv_ref[h])
    o_ref[0] = out.astype(BF16)


def _dsa_attention(aq, c, iq, ik, iw, bias_near, wuv_pad):
    B, S, DQ = aq.shape
    dl = c.shape[2]
    n_heads = DQ // dl
    d_att = wuv_pad.shape[2]
    topk = min(TOPK_MAX, S // 4)
    QB, KC = Q_BLOCK, KEY_CHUNK
    nkb = S // QB
    assert nkb % KC == 0
    return pl.pallas_call(
        functools.partial(_dsa_kernel, topk, n_heads),
        grid=(B, nkb),
        in_specs=[pl.BlockSpec((1, QB, DQ), lambda b, i: (b, i, 0)),
                  pl.BlockSpec((1, S, dl), lambda b, i: (b, 0, 0)),
                  pl.BlockSpec((1, QB, iq.shape[2]), lambda b, i: (b, i, 0)),
                  pl.BlockSpec((1, S, LANES), lambda b, i: (b, 0, 0)),
                  pl.BlockSpec((1, QB, LANES), lambda b, i: (b, i, 0)),
                  _const_spec(bias_near.shape), _const_spec(wuv_pad.shape)],
        out_specs=pl.BlockSpec((1, QB, d_att), lambda b, i: (b, i, 0)),
        out_shape=jax.ShapeDtypeStruct((B, S, d_att), BF16),
        scratch_shapes=[pltpu.VMEM(((nkb + KC - 1) * QB, dl), BF16),
                        pltpu.VMEM((nkb, QB, QB), I32), pltpu.VMEM((nkb, QB, QB), I16),
                        pltpu.VMEM((nkb, QB, QB), I16), pltpu.VMEM((nkb + KC - 1, QB, QB), F32),
                        pltpu.VMEM((n_heads, QB, QB), F32), pltpu.VMEM((n_heads, QB, QB), F32),
                        pltpu.VMEM((n_heads * QB, dl), F32)],
        compiler_params=_cparams("parallel", "arbitrary"),
        name="dsa",
    )(aq, c, iq, ik, iw, bias_near, wuv_pad)


def _t5_bucket(n):
    nf = jnp.maximum(n, 1).astype(F32)
    large = MAX_EXACT + (jnp.log(nf / MAX_EXACT) / math.log(MAX_DISTANCE / MAX_EXACT)
                         * (N_BUCKETS - MAX_EXACT)).astype(I32)
    large = jnp.minimum(large, N_BUCKETS - 1)
    return jnp.where(n < MAX_EXACT, n, large)


def _near_bias(rel_bias):
    QB = Q_BLOCK
    assert QB >= MAX_DISTANCE
    r = jnp.arange(QB)[:, None] - jnp.arange(QB)[None, :]
    far = rel_bias[N_BUCKETS - 1]
    diag = rel_bias[_t5_bucket(jnp.maximum(r, 0))] - far
    prev = rel_bias[_t5_bucket(r + QB)] - far
    return (jnp.stack([diag, prev]).transpose(0, 3, 1, 2) * LOG2E).astype(F32)


def _merge_kernel(alpha, D, h_ref, u_ref, r_ref, a_ref, gt_ref, wc_ref, wr_ref, wa_ref, wo_ref, g_ref, b_ref,
                  o_ref):
    y = gt_ref[:, 0:D].astype(F32) * _dot(u_ref[...], wc_ref[...])
    y = y + gt_ref[:, D:2 * D].astype(F32) * _dot(r_ref[...], wr_ref[...])
    y = y + gt_ref[:, 2 * D:3 * D].astype(F32) * _dot(a_ref[...], wa_ref[...])
    z = alpha * h_ref[...] + _dot(y.astype(BF16), wo_ref[...])
    o_ref[...] = _ln_rows(z, g_ref[...], b_ref[...])


def _merge(h, u, r, a, gt, wc, wr, wa, wo, g, b, alpha, tm):
    T, D = h.shape
    row = lambda n: pl.BlockSpec((tm, n), lambda i: (i, 0))
    return pl.pallas_call(
        functools.partial(_merge_kernel, alpha, D),
        grid=(T // tm,),
        in_specs=[row(D), row(u.shape[1]), row(r.shape[1]), row(a.shape[1]), row(3 * D),
                  _const_spec(wc.shape), _const_spec(wr.shape), _const_spec(wa.shape), _const_spec(wo.shape),
                  _const_spec((1, D)), _const_spec((1, D))],
        out_specs=row(D),
        out_shape=jax.ShapeDtypeStruct((T, D), F32),
        compiler_params=_cparams("parallel"),
        name="merge",
    )(h, u, r, a, gt, wc, wr, wa, wo, g.reshape(1, D), b.reshape(1, D))


def _split_bf16(x):
    hi = x.astype(BF16)
    return hi, (x - hi.astype(F32)).astype(BF16)


def _moe_kernel(alpha, n_exp, de, h_ref, rw_hi_ref, rw_lo_ref, rb_ref, wgu_ref, wdn_ref, wsgu_ref, wsdn_ref,
                g_ref, b_ref, o_ref, xb_ref, gate_ref, acc_ref):
    e = pl.program_id(1)
    lane = lax.broadcasted_iota(I32, gate_ref.shape, 1)

    @pl.when(e == 0)
    def _():
        x = h_ref[...]
        x_hi, x_lo = _split_bf16(x)
        xb_ref[...] = x_hi
        logits = _dot(x_hi, rw_hi_ref[...]) + (_dot(x_lo, rw_hi_ref[...]) + _dot(x_hi, rw_lo_ref[...]))
        scores = _sigmoid(logits)
        sb = scores + rb_ref[...]
        sel = jnp.zeros(sb.shape, jnp.bool_)
        for _ in range(TOPK_EXPERTS):
            m = jnp.max(sb, axis=-1, keepdims=True)
            first = jnp.min(jnp.where(sb == m, lane, n_exp), axis=-1, keepdims=True)
            hit = lane == first
            sel = sel | hit
            sb = jnp.where(hit, -jnp.inf, sb)
        w = jnp.where(sel, scores, 0.0)
        gate_ref[...] = w / jnp.sum(w, axis=-1, keepdims=True) * ROUTE_SCALE
        ds = wsdn_ref.shape[0]
        s = _dot(x_hi, wsgu_ref[...])
        a_s, u_s = s[:, :ds], s[:, ds:]
        acc_ref[...] = _dot((a_s * _sigmoid(a_s) * u_s).astype(BF16), wsdn_ref[...])

    xb = xb_ref[...]
    au = _dot(xb, wgu_ref[0])
    a, u = au[:, :de], au[:, de:]
    ge = jnp.sum(jnp.where(lane == e, gate_ref[...], 0.0), axis=-1, keepdims=True)
    hmid = a * _sigmoid(a) * u * ge
    acc_ref[...] += _dot(hmid.astype(BF16), wdn_ref[0])

    @pl.when(e == n_exp - 1)
    def _():
        o_ref[...] = _ln_rows(alpha * h_ref[...] + acc_ref[...], g_ref[...], b_ref[...])


def _moe(h, rw, rb, wgu, wdn, wsgu, wsdn, g, b, alpha, tm):
    T, D = h.shape
    n_exp, _, de2 = wgu.shape
    de = de2 // 2
    rw_hi, rw_lo = _split_bf16(rw)
    return pl.pallas_call(
        functools.partial(_moe_kernel, alpha, n_exp, de),
        grid=(T // tm, n_exp),
        in_specs=[pl.BlockSpec((tm, D), lambda i, e: (i, 0)),
                  _const_spec(rw.shape), _const_spec(rw.shape), _const_spec((1, n_exp)),
                  pl.BlockSpec((1, D, de2), lambda i, e: (e, 0, 0)),
                  pl.BlockSpec((1, de, D), lambda i, e: (e, 0, 0)),
                  _const_spec(wsgu.shape), _const_spec(wsdn.shape),
                  _const_spec((1, D)), _const_spec((1, D))],
        out_specs=pl.BlockSpec((tm, D), lambda i, e: (i, 0)),
        out_shape=jax.ShapeDtypeStruct((T, D), F32),
        scratch_shapes=[pltpu.VMEM((tm, D), BF16), pltpu.VMEM((tm, n_exp), F32), pltpu.VMEM((tm, D), F32)],
        compiler_params=_cparams("parallel", "arbitrary"),
        name="moe",
    )(h, rw_hi, rw_lo, rb.reshape(1, n_exp), wgu, wdn, wsgu, wsdn, g.reshape(1, D), b.reshape(1, D))


def _pack_in_weights(w, b, dims):
    dc, dr, daq, dl, diq, dg = dims
    sizes = [2 * dc, dr, dr, dr, dr, daq, dl, diq, IDX_HEAD_DIM, N_IDX_HEADS, dg]
    assert sum(sizes) == w.shape[1]
    offs = [0]
    for s in sizes:
        offs.append(offs[-1] + s)
    cols = [w[:, offs[i]:offs[i + 1]] for i in range(len(sizes))]
    bias = [b[offs[i]:offs[i + 1]] for i in range(len(sizes))]

    def pad(x, n):
        return jnp.pad(x, [(0, 0)] * (x.ndim - 1) + [(0, n - x.shape[-1])])

    cols[8], bias[8] = pad(cols[8], LANES), pad(bias[8], LANES)
    cols[9], bias[9] = pad(cols[9], LANES), pad(bias[9], LANES)
    return jnp.concatenate(cols, axis=1).astype(BF16), jnp.concatenate(bias).reshape(1, -1).astype(F32)


def kernel(x, ln_in_g, ln_in_b, w_in, b_in, conv_dw, conv_b, conv_ln_g, conv_ln_b, w_conv_proj, hgrn_gamma,
           hgrn_norm_g, w_rnn_proj, kv_norm_g, w_uv, w_att_proj, rel_bias, w_out, ln_mix_g, ln_mix_b,
           router_w, router_b, w_expert_gu, w_expert_dn, w_shared_gu, w_shared_dn, ln_ffn_g, ln_ffn_b):
    B, S, D = x.shape
    T = B * S
    depth = w_in.shape[0]
    dc = conv_dw.shape[2]
    dr = hgrn_gamma.shape[1]
    n_heads, dl, dv = w_uv.shape[1:]
    daq = n_heads * dl
    diq = N_IDX_HEADS * IDX_HEAD_DIM
    dims = (dc, dr, daq, dl, diq, 3 * D)
    alpha = (2 * depth) ** 0.25

    tm = min(512, T)
    lb_all = jnp.cumsum(jax.nn.softmax(hgrn_gamma.astype(F32), axis=0), axis=0)
    lb_all = lb_all - lb_all[0]
    bias_near = _near_bias(rel_bias)

    h = _layer_norm(x.reshape(T, D), ln_in_g, ln_in_b, tm)
    for l in range(depth):
        w_pack, b_pack = _pack_in_weights(w_in[l], b_in[l], dims)
        u, rq, rf, ri, rg, aq, c, iq, ik, iw, gt = _in_projection(h, w_pack, b_pack, kv_norm_g[l], dims, tm)

        def seq(t):
            return t.reshape(B, S, t.shape[-1])

        y_conv = _conformer_conv(seq(u), conv_dw[l], conv_b[l], conv_ln_g[l], conv_ln_b[l], min(512, S))
        y_rnn = _hgrn2(seq(rq), seq(rf), seq(ri), seq(rg), lb_all[l], hgrn_norm_g[l], min(256, S))
        wuv_pad = jnp.zeros((n_heads, dl, n_heads * dv), F32)
        for hh in range(n_heads):
            wuv_pad = wuv_pad.at[hh, :, hh * dv:(hh + 1) * dv].set(w_uv[l, hh])
        y_att = _dsa_attention(seq(aq), seq(c), seq(iq), seq(ik), seq(iw), bias_near, wuv_pad.astype(BF16))
        h = _merge(h, y_conv.reshape(T, -1), y_rnn.reshape(T, -1), y_att.reshape(T, -1), gt,
                   w_conv_proj[l].astype(BF16), w_rnn_proj[l].astype(BF16), w_att_proj[l].astype(BF16),
                   w_out[l].astype(BF16), ln_mix_g[l], ln_mix_b[l], alpha, tm)
        h = _moe(h, router_w[l], router_b[l], w_expert_gu[l].astype(BF16), w_expert_dn[l].astype(BF16),
                 w_shared_gu[l].astype(BF16), w_shared_dn[l].astype(BF16), ln_ffn_g[l], ln_ffn_b[l], alpha,
                 min(1024, T))
    return h.reshape(B, S, D)
```

```python
import functools
import math

import jax
import jax.numpy as jnp
from jax import lax
from jax.experimental import pallas as pl
from jax.experimental.pallas import tpu as pltpu

N_RNN_HEADS = 4
N_IDX_HEADS = 4
IDX_HEAD_DIM = 64
TOPK_MAX = 256
N_BUCKETS = 32
MAX_EXACT = N_BUCKETS // 2
MAX_DISTANCE = 128
TOPK_EXPERTS = 4
ROUTE_SCALE = 2.5
LN_EPS = 1e-5

LANES = 128
VMEM_LIMIT_BYTES = 56 * 1024 * 1024

Q_BLOCK = 128
KEY_CHUNK = 4
RNN_CHUNK = 64
RNN_SUB = 16
NEG_BIG = -1e30
INT_MIN = -2147483648
I16_MIN = -32768
LOG2E = math.log2(math.e)

BF16 = jnp.bfloat16
F32 = jnp.float32
I16 = jnp.int16
I32 = jnp.int32


def _cparams(*sem):
    return pltpu.CompilerParams(dimension_semantics=sem, vmem_limit_bytes=VMEM_LIMIT_BYTES)


def _const_spec(shape):
    nd = len(shape)
    return pl.BlockSpec(shape, lambda *_: (0,) * nd, pipeline_mode=pl.Buffered(1))


def _ln_rows(x, g, b):
    mu = jnp.mean(x, axis=-1, keepdims=True)
    xc = x - mu
    var = jnp.mean(xc * xc, axis=-1, keepdims=True)
    return xc * lax.rsqrt(var + LN_EPS) * g + b


def _sigmoid(x):
    return 1.0 / (1.0 + jnp.exp(-x))


def _dot(a, b):
    return jnp.dot(a, b, preferred_element_type=F32)


def _dot_nt(a, b):
    return lax.dot_general(a, b, (((1,), (1,)), ((), ())), preferred_element_type=F32)


def _dot_tn(a, b):
    return lax.dot_general(a, b, (((0,), (0,)), ((), ())), preferred_element_type=F32)


def _ln_kernel(x_ref, g_ref, b_ref, o_ref):
    o_ref[...] = _ln_rows(x_ref[...], g_ref[...], b_ref[...])


def _layer_norm(x, g, b, tm):
    T, D = x.shape
    return pl.pallas_call(
        _ln_kernel,
        grid=(T // tm,),
        in_specs=[pl.BlockSpec((tm, D), lambda i: (i, 0)), _const_spec((1, D)), _const_spec((1, D))],
        out_specs=pl.BlockSpec((tm, D), lambda i: (i, 0)),
        out_shape=jax.ShapeDtypeStruct((T, D), F32),
        compiler_params=_cparams("parallel"),
        name="ln_in",
    )(x, g.reshape(1, D), b.reshape(1, D))


def _proj_kernel(dims, h_ref, w_ref, b_ref, kvg_ref,
                 u_ref, rq_ref, rf_ref, ri_ref, rg_ref, aq_ref, c_ref, iq_ref, ikw_ref, gt_ref):
    dc, dr, daq, dl, diq, dg = dims
    x = h_ref[...].astype(BF16)
    off = [0]

    def seg(n):
        o = off[0]
        off[0] = o + n
        return _dot(x, w_ref[:, o:o + n]) + b_ref[:, o:o + n]

    a = seg(dc)
    gate = seg(dc)
    u_ref[...] = (a * _sigmoid(gate)).astype(BF16)
    rq_ref[...] = seg(dr).astype(BF16)
    rf_ref[...] = seg(dr)
    ri_ref[...] = seg(dr).astype(BF16)
    rg_ref[...] = _sigmoid(seg(dr)).astype(BF16)
    scale = dl ** -0.5 * LOG2E
    for j in range(daq // 512):
        aq_ref[:, j * 512:(j + 1) * 512] = (seg(512) * scale).astype(BF16)
    c = seg(dl)
    c = c * lax.rsqrt(jnp.mean(c * c, axis=-1, keepdims=True) + LN_EPS) * kvg_ref[...]
    c_ref[...] = c.astype(BF16)
    iq_ref[...] = seg(diq).astype(BF16)
    ikw_ref[...] = seg(LANES)
    for j in range(dg // 512):
        gt_ref[:, j * 512:(j + 1) * 512] = _sigmoid(seg(512)).astype(BF16)


def _in_projection(h, w, b, kv_g, dims, tm):
    T, D = h.shape
    dc, dr, daq, dl, diq, dg = dims
    n_pad = w.shape[1]
    widths = [(dc, BF16), (dr, BF16), (dr, F32), (dr, BF16), (dr, BF16), (daq, BF16), (dl, BF16),
              (diq, BF16), (LANES, F32), (dg, BF16)]
    return pl.pallas_call(
        functools.partial(_proj_kernel, dims),
        grid=(T // tm,),
        in_specs=[pl.BlockSpec((tm, D), lambda i: (i, 0)), _const_spec((D, n_pad)),
                  _const_spec((1, n_pad)), _const_spec((1, dl))],
        out_specs=[pl.BlockSpec((tm, n), lambda i: (i, 0)) for n, _ in widths],
        out_shape=[jax.ShapeDtypeStruct((T, n), dt) for n, dt in widths],
        compiler_params=_cparams("parallel"),
        name="in_proj",
    )(h, w, b, kv_g.reshape(1, dl))


def _conv_kernel(width, ts, sub, u_ref, halo_ref, dw_ref, db_ref, g_ref, b_ref, o_ref, buf_ref):
    i = pl.program_id(1)
    halo = halo_ref[0].astype(F32)
    buf_ref[0:32, :] = jnp.where(i > 0, halo, 0.0)
    buf_ref[32:32 + ts, :] = u_ref[0].astype(F32)
    base = 32 - (width - 1)
    for s in range(ts // sub):
        acc = jnp.zeros((sub, u_ref.shape[2]), F32) + db_ref[...]
        for j in range(width):
            r0 = base + s * sub + j
            acc = acc + buf_ref[r0:r0 + sub, :] * dw_ref[j:j + 1, :]
        y = _ln_rows(acc, g_ref[...], b_ref[...])
        o_ref[0, s * sub:(s + 1) * sub, :] = (y * _sigmoid(y)).astype(BF16)


def _conformer_conv(u, dw, db, g, b, ts):
    B, S, C = u.shape
    width = dw.shape[0]
    assert width - 1 <= 32
    sub = min(64, ts)
    hb = ts // 32
    return pl.pallas_call(
        functools.partial(_conv_kernel, width, ts, sub),
        grid=(B, S // ts),
        in_specs=[pl.BlockSpec((1, ts, C), lambda bi, i: (bi, i, 0)),
                  pl.BlockSpec((1, 32, C), lambda bi, i: (bi, jnp.maximum(i * hb - 1, 0), 0)),
                  _const_spec((width, C)), _const_spec((1, C)), _const_spec((1, C)), _const_spec((1, C))],
        out_specs=pl.BlockSpec((1, ts, C), lambda bi, i: (bi, i, 0)),
        out_shape=jax.ShapeDtypeStruct((B, S, C), BF16),
        scratch_shapes=[pltpu.VMEM((ts + 32, C), F32)],
        compiler_params=_cparams("parallel", "arbitrary"),
        name="conv",
    )(u, u, dw, db.reshape(1, C), g.reshape(1, C), b.reshape(1, C))


def _hgrn_chunk(q, z, v, lb, st):
    C, Dh = q.shape
    ls = jnp.minimum(z, 0.0) - jnp.log1p(jnp.exp(-jnp.abs(z)))
    a_ = jnp.log(lb)
    b_ = jnp.log1p(-lb) + ls
    lf = jnp.maximum(a_, b_) + jnp.log1p(jnp.exp(-jnp.abs(a_ - b_)))
    kk = (1.0 - lb) * _sigmoid(-z)

    row = lax.broadcasted_iota(I32, (C, C), 0)
    col = lax.broadcasted_iota(I32, (C, C), 1)
    tri = (col <= row).astype(F32)
    A = jnp.dot(tri, lf, preferred_element_type=F32, precision=lax.Precision.HIGHEST)

    o = _dot_nt((q * jnp.exp(A)).astype(BF16), st.astype(BF16))

    rmod = lax.broadcasted_iota(I32, (C, Dh), 0) % RNN_SUB
    parts = []
    for d in range(RNN_SUB):
        if d == 0:
            e = q * kk
        else:
            a_s = pltpu.roll(A, d, axis=0)
            k_s = pltpu.roll(kk, d, axis=0)
            e = jnp.exp(jnp.where(rmod >= d, A - a_s, NEG_BIG)) * q * k_s
        parts.append(e.astype(BF16))
    ones = jnp.ones((Dh, LANES), BF16)
    rs = _dot(jnp.concatenate(parts, axis=0), ones)
    P = jnp.zeros((C, C), F32)
    for d in range(RNN_SUB):
        P = P + jnp.where(col == row - d, rs[d * C:(d + 1) * C, :C], 0.0)

    rowd = lax.broadcasted_iota(I32, (C, Dh), 0)
    blocks = [jnp.zeros((RNN_SUB, C), F32)]
    for i in range(1, C // RNN_SUB):
        r0 = i * RNN_SUB
        a_i = A[r0 - 1:r0, :]
        qi = q[r0:r0 + RNN_SUB] * jnp.exp(A[r0:r0 + RNN_SUB] - a_i)
        ki = jnp.where(rowd < r0, kk * jnp.exp(jnp.minimum(a_i - A, 0.0)), 0.0)
        blocks.append(_dot_nt(qi.astype(BF16), ki.astype(BF16)))
    P = P + jnp.concatenate(blocks, axis=0)
    o = o + _dot(P.astype(BF16), v.astype(BF16))

    a_last = A[C - 1:C, :]
    kd = kk * jnp.exp(a_last - A)
    st_new = st * jnp.exp(a_last) + _dot_tn(v.astype(BF16), kd.astype(BF16))
    return o, st_new


def _hgrn_kernel(n_chunks, q_ref, z_ref, v_ref, g_ref, lb_ref, ng_ref, o_ref, st_ref):
    @pl.when(pl.program_id(2) == 0)
    def _():
        st_ref[...] = jnp.zeros_like(st_ref)

    lb = lb_ref[...]
    st = st_ref[...]
    C = RNN_CHUNK
    for c in range(n_chunks):
        sl = slice(c * C, (c + 1) * C)
        o, st = _hgrn_chunk(q_ref[0, sl, :].astype(F32), z_ref[0, sl, :], v_ref[0, sl, :].astype(F32), lb, st)
        o = o * lax.rsqrt(jnp.mean(o * o, axis=-1, keepdims=True) + LN_EPS) * ng_ref[...]
        o_ref[0, sl, :] = (o * g_ref[0, sl, :].astype(F32)).astype(BF16)
    st_ref[...] = st


def _hgrn2(rq, rf, ri, rg, lb, norm_g, rows):
    B, S, DR = rq.shape
    H = N_RNN_HEADS
    Dh = DR // H
    blk = pl.BlockSpec((1, rows, Dh), lambda b, h, i: (b, i, h))
    vec = pl.BlockSpec((1, Dh), lambda b, h, i: (0, h))
    return pl.pallas_call(
        functools.partial(_hgrn_kernel, rows // RNN_CHUNK),
        grid=(B, H, S // rows),
        in_specs=[blk, blk, blk, blk, vec, vec],
        out_specs=blk,
        out_shape=jax.ShapeDtypeStruct((B, S, DR), BF16),
        scratch_shapes=[pltpu.VMEM((Dh, Dh), F32)],
        compiler_params=_cparams("parallel", "parallel", "arbitrary"),
        name="hgrn2",
    )(rq, rf, ri, rg, lb.reshape(1, DR), norm_g.reshape(1, DR))


def _dsa_kernel(topk, n_heads, aq_ref, c_ref, iq_ref, ik_ref, ikw_q_ref, bias_ref, wuv_ref, o_ref,
                cpad_ref, key_ref, hi_ref, lo_ref, mask_ref, m_ref, l_ref, acc_ref):
    QB, KC = Q_BLOCK, KEY_CHUNK
    PAD = KC - 1
    qi = pl.program_id(1)
    n_chunks = qi // KC + 1
    row = lax.broadcasted_iota(I32, (QB, QB), 0)
    col = lax.broadcasted_iota(I32, (QB, QB), 1)
    dl = c_ref.shape[2]

    @pl.when(qi == 0)
    def _():
        cpad_ref[0:PAD * QB, :] = jnp.zeros((PAD * QB, dl), BF16)
        cpad_ref[PAD * QB:, :] = c_ref[0]

    iq = iq_ref[0]
    q_st = jnp.concatenate([iq[:, h * IDX_HEAD_DIM:(h + 1) * IDX_HEAD_DIM] for h in range(N_IDX_HEADS)], axis=0)
    ikw_q = ikw_q_ref[0]
    wb = [jnp.broadcast_to(ikw_q[:, IDX_HEAD_DIM + h:IDX_HEAD_DIM + h + 1], (QB, QB))
          for h in range(N_IDX_HEADS)]

    def idx_keys(ci):
        rows = pl.ds(pl.multiple_of(ci * (KC * QB), KC * QB), KC * QB)
        return ik_ref[0, rows, :][:, :IDX_HEAD_DIM].astype(BF16)
    dcr = col - row

    def score_body(ci, carry):
        kj = idx_keys(ci)
        s4 = _dot_nt(q_st, kj)
        for b in range(KC):
            j = ci * KC + b
            sc = jnp.zeros((QB, QB), F32)
            for h in range(N_IDX_HEADS):
                sc = sc + wb[h] * jnp.maximum(s4[h * QB:(h + 1) * QB, b * QB:(b + 1) * QB], 0.0)
            sc = sc + 0.0
            bits = pltpu.bitcast(sc, I32)
            key = bits ^ ((bits >> 31) & 0x7FFFFFFF)
            key = jnp.where(dcr <= (qi - j) * QB, key, INT_MIN)
            key_ref[j] = key
        return carry

    lax.fori_loop(0, n_chunks, score_body, 0)

    w_rows = ikw_q.T[IDX_HEAD_DIM:IDX_HEAD_DIM + 8, :]
    dcr_t = row - col

    def score_t_body(ci, carry):
        kj = idx_keys(ci)
        s4 = _dot_nt(kj, q_st)
        for b in range(KC):
            j = ci * KC + b
            sc = jnp.zeros((QB, QB), F32)
            for h in range(N_IDX_HEADS):
                sc = sc + w_rows[h:h + 1, :] * jnp.maximum(s4[b * QB:(b + 1) * QB, h * QB:(h + 1) * QB], 0.0)
            sc = sc + 0.0
            bits = pltpu.bitcast(sc, I32)
            key = bits ^ ((bits >> 31) & 0x7FFFFFFF)
            key = jnp.where(dcr_t <= (qi - j) * QB, key, INT_MIN)
            hi_ref[j] = (key >> 16).astype(I16)
            lo_ref[j] = ((key & 0xFFFF) + I16_MIN).astype(I16)
        return carry

    lax.fori_loop(0, n_chunks, score_t_body, 0)

    def count_rows(acc):
        return jnp.sum(acc.astype(F32), axis=0, keepdims=True)

    def radix16(src_ref, need):
        def bit_body(it, cur):
            cand = cur + lax.shift_left(jnp.int32(1), 15 - it)
            cand16 = jnp.broadcast_to(cand, (QB, QB)).astype(I16)

            def cnt_body(ci, acc):
                for b in range(KC):
                    acc = acc + jnp.where(src_ref[ci * KC + b] >= cand16, jnp.int16(1), jnp.int16(0))
                return acc

            acc = lax.fori_loop(0, n_chunks, cnt_body, jnp.zeros((QB, QB), I16))
            return jnp.where(count_rows(acc) >= need, cand, cur)

        return lax.fori_loop(0, 16, bit_body, jnp.full((1, QB), I16_MIN, I32))

    t_hi = radix16(hi_ref, float(topk))
    t_hi16 = jnp.broadcast_to(t_hi, (QB, QB)).astype(I16)

    def mid_body(ci, acc):
        for b in range(KC):
            j = ci * KC + b
            hi = hi_ref[j]
            acc = acc + jnp.where(hi > t_hi16, jnp.int16(1), jnp.int16(0))
            lo_ref[j] = jnp.where(hi == t_hi16, lo_ref[j], jnp.int16(I16_MIN))
        return acc

    n_hi_gt = count_rows(lax.fori_loop(0, n_chunks, mid_body, jnp.zeros((QB, QB), I16)))
    t_lo = radix16(lo_ref, topk - n_hi_gt)
    thr = t_hi * 65536 + (t_lo - I16_MIN)
    thr = jnp.maximum(thr, INT_MIN + 1)
    thr_b = jnp.broadcast_to(thr, (QB, QB)).T

    for p in range(PAD):
        mask_ref[p] = jnp.full((QB, QB), NEG_BIG, F32)

    def mask_body(ci, acc):
        for b in range(KC):
            j = ci * KC + b
            ge = key_ref[j] >= thr_b
            mask_ref[PAD + j] = jnp.where(ge, 0.0, NEG_BIG)
            acc = acc + jnp.where(ge, 1, 0)
        return acc

    n_sel = jnp.sum(lax.fori_loop(0, n_chunks, mask_body, jnp.zeros((QB, QB), I32)), axis=1, keepdims=True)

    @pl.when(jnp.max(n_sel) > topk)
    def _():
        def gt_body(j, acc):
            return acc + jnp.where(key_ref[j] > thr_b, 1, 0)

        n_gt = jnp.sum(lax.fori_loop(0, qi + 1, gt_body, jnp.zeros((QB, QB), I32)), axis=1, keepdims=True)
        need_b = jnp.broadcast_to((topk - n_gt).astype(F32), (QB, QB))
        cum_rhs = jnp.concatenate([(row <= col).astype(BF16), jnp.ones((QB, QB), BF16)], axis=1)

        def tie_body(j, carry):
            key = key_ref[j]
            eq = key == thr_b
            cs = _dot(jnp.where(eq, 1.0, 0.0).astype(BF16), cum_rhs)
            sel = (key > thr_b) | (eq & (carry + cs[:, :QB] <= need_b))
            mask_ref[PAD + j] = jnp.where(sel, 0.0, NEG_BIG)
            return carry + cs[:, QB:]

        lax.fori_loop(0, qi + 1, tie_body, jnp.zeros((QB, QB), F32))

    q_all = jnp.concatenate([aq_ref[0, :, h * dl:(h + 1) * dl] for h in range(n_heads)], axis=0)
    n_far = qi // KC

    def logits(start, near):
        cch = cpad_ref[pl.ds(pl.multiple_of(start * QB, QB), KC * QB), :]
        lg = _dot_nt(q_all, cch)
        out = []
        for h in range(n_heads):
            blks = []
            for b in range(KC):
                x = lg[h * QB:(h + 1) * QB, b * QB:(b + 1) * QB] + mask_ref[start + b]
                if near and b == KC - 1:
                    x = x + bias_ref[0, h]
                if near and b == KC - 2:
                    x = x + bias_ref[1, h]
                blks.append(x)
            out.append(blks)
        return cch, out

    def att_step(start, near):
        cch, lg = logits(start, near)
        rows, alphas = [], []
        for h in range(n_heads):
            m_old = m_ref[h]
            mx = lg[h][0]
            for b in range(1, KC):
                mx = jnp.maximum(mx, lg[h][b])
            m_new = jnp.maximum(m_old, jnp.broadcast_to(jnp.max(mx, axis=1, keepdims=True), (QB, QB)))
            alpha = jnp.exp2(m_old - m_new)
            ps = [jnp.exp2(lg[h][b] - m_new) for b in range(KC)]
            s = ps[0]
            for b in range(1, KC):
                s = s + ps[b]
            l_ref[h] = alpha * l_ref[h] + s
            m_ref[h] = m_new
            rows.append(jnp.concatenate([p.astype(BF16) for p in ps], axis=1))
            alphas.append(alpha)
        pv = _dot(jnp.concatenate(rows, axis=0), cch)
        for h in range(n_heads):
            sl = slice(h * QB, (h + 1) * QB)
            acc_ref[sl, :] = alphas[h] * acc_ref[sl, :] + pv[sl, :]

    m_ref[...] = jnp.full(m_ref.shape, NEG_BIG, F32)
    l_ref[...] = jnp.zeros(l_ref.shape, F32)
    acc_ref[...] = jnp.zeros(acc_ref.shape, F32)

    def far_body(m, carry):
        att_step(qi - KC * (m + 1), False)
        return carry

    lax.fori_loop(0, n_far, far_body, 0)
    att_step(qi, True)

    out = jnp.zeros((QB, o_ref.shape[2]), F32)
    for h in range(n_heads):
        oh = acc_ref[h * QB:(h + 1) * QB, :] / jnp.sum(l_ref[h], axis=1, keepdims=True)
        out = out + _dot(oh.astype(BF16), wuv_ref[h])
    o_ref[0] = out.astype(BF16)


def _dsa_attention(aq, c, iq, ikw, bias_near, wuv_pad):
    B, S, DQ = aq.shape
    dl = c.shape[2]
    assert dl == Q_BLOCK
    n_heads = DQ // dl
    d_att = wuv_pad.shape[2]
    topk = min(TOPK_MAX, S // 4)
    QB, KC = Q_BLOCK, KEY_CHUNK
    nkb = S // QB
    assert nkb % KC == 0
    return pl.pallas_call(
        functools.partial(_dsa_kernel, topk, n_heads),
        grid=(B, nkb),
        in_specs=[pl.BlockSpec((1, QB, DQ), lambda b, i: (b, i, 0)),
                  pl.BlockSpec((1, S, dl), lambda b, i: (b, 0, 0)),
                  pl.BlockSpec((1, QB, iq.shape[2]), lambda b, i: (b, i, 0)),
                  pl.BlockSpec((1, S, LANES), lambda b, i: (b, 0, 0)),
                  pl.BlockSpec((1, QB, LANES), lambda b, i: (b, i, 0)),
                  _const_spec(bias_near.shape), _const_spec(wuv_pad.shape)],
        out_specs=pl.BlockSpec((1, QB, d_att), lambda b, i: (b, i, 0)),
        out_shape=jax.ShapeDtypeStruct((B, S, d_att), BF16),
        scratch_shapes=[pltpu.VMEM(((nkb + KC - 1) * QB, dl), BF16),
                        pltpu.VMEM((nkb, QB, QB), I32), pltpu.VMEM((nkb, QB, QB), I16),
                        pltpu.VMEM((nkb, QB, QB), I16), pltpu.VMEM((nkb + KC - 1, QB, QB), F32),
                        pltpu.VMEM((n_heads, QB, QB), F32), pltpu.VMEM((n_heads, QB, QB), F32),
                        pltpu.VMEM((n_heads * QB, dl), F32)],
        compiler_params=_cparams("parallel", "arbitrary"),
        name="dsa",
    )(aq, c, iq, ikw, ikw, bias_near, wuv_pad)


def _t5_bucket(n):
    nf = jnp.maximum(n, 1).astype(F32)
    large = MAX_EXACT + (jnp.log(nf / MAX_EXACT) / math.log(MAX_DISTANCE / MAX_EXACT)
                         * (N_BUCKETS - MAX_EXACT)).astype(I32)
    large = jnp.minimum(large, N_BUCKETS - 1)
    return jnp.where(n < MAX_EXACT, n, large)


def _near_bias(rel_bias):
    QB = Q_BLOCK
    assert QB >= MAX_DISTANCE
    r = jnp.arange(QB)[:, None] - jnp.arange(QB)[None, :]
    far = rel_bias[N_BUCKETS - 1]
    diag = rel_bias[_t5_bucket(jnp.maximum(r, 0))] - far
    prev = rel_bias[_t5_bucket(r + QB)] - far
    return (jnp.stack([diag, prev]).transpose(0, 3, 1, 2) * LOG2E).astype(F32)


def _merge_kernel(alpha, D, h_ref, u_ref, r_ref, a_ref, gt_ref, wc_ref, wr_ref, wa_ref, wo_ref, g_ref, b_ref,
                  o_ref):
    y = gt_ref[:, 0:D].astype(F32) * _dot(u_ref[...], wc_ref[...])
    y = y + gt_ref[:, D:2 * D].astype(F32) * _dot(r_ref[...], wr_ref[...])
    y = y + gt_ref[:, 2 * D:3 * D].astype(F32) * _dot(a_ref[...], wa_ref[...])
    z = alpha * h_ref[...] + _dot(y.astype(BF16), wo_ref[...])
    o_ref[...] = _ln_rows(z, g_ref[...], b_ref[...])


def _merge(h, u, r, a, gt, wc, wr, wa, wo, g, b, alpha, tm):
    T, D = h.shape
    row = lambda n: pl.BlockSpec((tm, n), lambda i: (i, 0))
    return pl.pallas_call(
        functools.partial(_merge_kernel, alpha, D),
        grid=(T // tm,),
        in_specs=[row(D), row(u.shape[1]), row(r.shape[1]), row(a.shape[1]), row(3 * D),
                  _const_spec(wc.shape), _const_spec(wr.shape), _const_spec(wa.shape), _const_spec(wo.shape),
                  _const_spec((1, D)), _const_spec((1, D))],
        out_specs=row(D),
        out_shape=jax.ShapeDtypeStruct((T, D), F32),
        compiler_params=_cparams("parallel"),
        name="merge",
    )(h, u, r, a, gt, wc, wr, wa, wo, g.reshape(1, D), b.reshape(1, D))


def _split_bf16(x):
    hi = x.astype(BF16)
    return hi, (x - hi.astype(F32)).astype(BF16)


def _moe_kernel(alpha, n_exp, de, h_ref, rw_hi_ref, rw_lo_ref, rb_ref, wgu_ref, wdn_ref, wsgu_ref, wsdn_ref,
                g_ref, b_ref, o_ref, xb_ref, gate_ref, acc_ref):
    e = pl.program_id(1)
    lane = lax.broadcasted_iota(I32, gate_ref.shape, 1)

    @pl.when(e == 0)
    def _():
        x = h_ref[...]
        x_hi, x_lo = _split_bf16(x)
        xb_ref[...] = x_hi
        logits = _dot(x_hi, rw_hi_ref[...]) + (_dot(x_lo, rw_hi_ref[...]) + _dot(x_hi, rw_lo_ref[...]))
        scores = _sigmoid(logits)
        sb = scores + rb_ref[...]
        sel = jnp.zeros(sb.shape, jnp.bool_)
        for _ in range(TOPK_EXPERTS):
            m = jnp.max(sb, axis=-1, keepdims=True)
            first = jnp.min(jnp.where(sb == m, lane, n_exp), axis=-1, keepdims=True)
            hit = lane == first
            sel = sel | hit
            sb = jnp.where(hit, -jnp.inf, sb)
        w = jnp.where(sel, scores, 0.0)
        gate_ref[...] = w / jnp.sum(w, axis=-1, keepdims=True) * ROUTE_SCALE
        ds = wsdn_ref.shape[0]
        s = _dot(x_hi, wsgu_ref[...])
        a_s, u_s = s[:, :ds], s[:, ds:]
        acc_ref[...] = _dot((a_s * _sigmoid(a_s) * u_s).astype(BF16), wsdn_ref[...])

    xb = xb_ref[...]
    au = _dot(xb, wgu_ref[0])
    a, u = au[:, :de], au[:, de:]
    ge = jnp.sum(jnp.where(lane == e, gate_ref[...], 0.0), axis=-1, keepdims=True)
    hmid = a * _sigmoid(a) * u * ge
    acc_ref[...] += _dot(hmid.astype(BF16), wdn_ref[0])

    @pl.when(e == n_exp - 1)
    def _():
        o_ref[...] = _ln_rows(alpha * h_ref[...] + acc_ref[...], g_ref[...], b_ref[...])


def _moe(h, rw, rb, wgu, wdn, wsgu, wsdn, g, b, alpha, tm):
    T, D = h.shape
    n_exp, _, de2 = wgu.shape
    de = de2 // 2
    rw_hi, rw_lo = _split_bf16(rw)
    return pl.pallas_call(
        functools.partial(_moe_kernel, alpha, n_exp, de),
        grid=(T // tm, n_exp),
        in_specs=[pl.BlockSpec((tm, D), lambda i, e: (i, 0)),
                  _const_spec(rw.shape), _const_spec(rw.shape), _const_spec((1, n_exp)),
                  pl.BlockSpec((1, D, de2), lambda i, e: (e, 0, 0)),
                  pl.BlockSpec((1, de, D), lambda i, e: (e, 0, 0)),
                  _const_spec(wsgu.shape), _const_spec(wsdn.shape),
                  _const_spec((1, D)), _const_spec((1, D))],
        out_specs=pl.BlockSpec((tm, D), lambda i, e: (i, 0)),
        out_shape=jax.ShapeDtypeStruct((T, D), F32),
        scratch_shapes=[pltpu.VMEM((tm, D), BF16), pltpu.VMEM((tm, n_exp), F32), pltpu.VMEM((tm, D), F32)],
        compiler_params=_cparams("parallel", "arbitrary"),
        name="moe",
    )(h, rw_hi, rw_lo, rb.reshape(1, n_exp), wgu, wdn, wsgu, wsdn, g.reshape(1, D), b.reshape(1, D))


def _pack_in_weights(w, b, dims):
    dc, dr, daq, dl, diq, dg = dims
    aligned = [2 * dc, dr, dr, dr, dr, daq, dl, diq]
    assert all(s % LANES == 0 for s in aligned) and IDX_HEAD_DIM + N_IDX_HEADS <= LANES
    split = sum(aligned) + IDX_HEAD_DIM + N_IDX_HEADS
    assert split + dg == w.shape[1]
    pad = LANES - IDX_HEAD_DIM - N_IDX_HEADS
    wb = w.astype(BF16)
    w_pack = jnp.concatenate([wb[:, :split], jnp.zeros((w.shape[0], pad), BF16), wb[:, split:]], axis=1)
    b_pack = jnp.concatenate([b[:split], jnp.zeros((pad,), b.dtype), b[split:]])
    return w_pack, b_pack.reshape(1, -1).astype(F32)


def kernel(x, ln_in_g, ln_in_b, w_in, b_in, conv_dw, conv_b, conv_ln_g, conv_ln_b, w_conv_proj, hgrn_gamma,
           hgrn_norm_g, w_rnn_proj, kv_norm_g, w_uv, w_att_proj, rel_bias, w_out, ln_mix_g, ln_mix_b,
           router_w, router_b, w_expert_gu, w_expert_dn, w_shared_gu, w_shared_dn, ln_ffn_g, ln_ffn_b):
    B, S, D = x.shape
    T = B * S
    depth = w_in.shape[0]
    dc = conv_dw.shape[2]
    dr = hgrn_gamma.shape[1]
    n_heads, dl, dv = w_uv.shape[1:]
    daq = n_heads * dl
    diq = N_IDX_HEADS * IDX_HEAD_DIM
    dims = (dc, dr, daq, dl, diq, 3 * D)
    alpha = (2 * depth) ** 0.25

    tm = min(512, T)
    lb_all = jnp.cumsum(jax.nn.softmax(hgrn_gamma.astype(F32), axis=0), axis=0)
    lb_all = lb_all - lb_all[0]
    bias_near = _near_bias(rel_bias)

    h = _layer_norm(x.reshape(T, D), ln_in_g, ln_in_b, tm)
    for l in range(depth):
        w_pack, b_pack = _pack_in_weights(w_in[l], b_in[l], dims)
        u, rq, rf, ri, rg, aq, c, iq, ikw, gt = _in_projection(h, w_pack, b_pack, kv_norm_g[l], dims, tm)

        def seq(t):
            return t.reshape(B, S, t.shape[-1])

        y_conv = _conformer_conv(seq(u), conv_dw[l], conv_b[l], conv_ln_g[l], conv_ln_b[l], min(512, S))
        y_rnn = _hgrn2(seq(rq), seq(rf), seq(ri), seq(rg), lb_all[l], hgrn_norm_g[l], min(256, S))
        wuv_pad = jnp.zeros((n_heads, dl, n_heads * dv), F32)
        for hh in range(n_heads):
            wuv_pad = wuv_pad.at[hh, :, hh * dv:(hh + 1) * dv].set(w_uv[l, hh])
        y_att = _dsa_attention(seq(aq), seq(c), seq(iq), seq(ikw), bias_near, wuv_pad.astype(BF16))
        h = _merge(h, y_conv.reshape(T, -1), y_rnn.reshape(T, -1), y_att.reshape(T, -1), gt,
                   w_conv_proj[l].astype(BF16), w_rnn_proj[l].astype(BF16), w_att_proj[l].astype(BF16),
                   w_out[l].astype(BF16), ln_mix_g[l], ln_mix_b[l], alpha, tm)
        h = _moe(h, router_w[l], router_b[l], w_expert_gu[l].astype(BF16), w_expert_dn[l].astype(BF16),
                 w_shared_gu[l].astype(BF16), w_shared_dn[l].astype(BF16), ln_ffn_g[l], ln_ffn_b[l], alpha,
                 min(1024, T))
    return h.reshape(B, S, D)
```

```python
import functools
import math

import jax
import jax.numpy as jnp
from jax import lax
from jax.experimental import pallas as pl
from jax.experimental.pallas import tpu as pltpu

N_RNN_HEADS = 4
N_IDX_HEADS = 4
IDX_HEAD_DIM = 64
TOPK_MAX = 256
N_BUCKETS = 32
MAX_EXACT = N_BUCKETS // 2
MAX_DISTANCE = 128
TOPK_EXPERTS = 4
ROUTE_SCALE = 2.5
LN_EPS = 1e-5

LANES = 128
SUBLANES = 8
VMEM_LIMIT_BYTES = 56 * 1024 * 1024

Q_BLOCK = 128
KEY_CHUNK = 4
RNN_CHUNK = 64
RNN_SUB = 16
RNN_UNROLL = 16
NEG_BIG = -1e30
INT_MIN = -2147483648
I16_MIN = -32768
LOG2E = math.log2(math.e)

BF16 = jnp.bfloat16
F32 = jnp.float32
I16 = jnp.int16
I32 = jnp.int32


def _cparams(*sem):
    return pltpu.CompilerParams(dimension_semantics=sem, vmem_limit_bytes=VMEM_LIMIT_BYTES)


def _const_spec(shape):
    nd = len(shape)
    return pl.BlockSpec(shape, lambda *_: (0,) * nd, pipeline_mode=pl.Buffered(1))


def _ln_rows(x, g, b):
    mu = jnp.mean(x, axis=-1, keepdims=True)
    xc = x - mu
    var = jnp.mean(xc * xc, axis=-1, keepdims=True)
    return xc * lax.rsqrt(var + LN_EPS) * g + b


def _sigmoid(x):
    return 1.0 / (1.0 + jnp.exp(-x))


def _dot(a, b):
    return jnp.dot(a, b, preferred_element_type=F32)


def _dot_nt(a, b):
    return lax.dot_general(a, b, (((1,), (1,)), ((), ())), preferred_element_type=F32)


def _dot_tn(a, b):
    return lax.dot_general(a, b, (((0,), (0,)), ((), ())), preferred_element_type=F32)


def _ln_kernel(x_ref, g_ref, b_ref, o_ref):
    o_ref[...] = _ln_rows(x_ref[...], g_ref[...], b_ref[...])


def _layer_norm(x, g, b, tm):
    T, D = x.shape
    return pl.pallas_call(
        _ln_kernel,
        grid=(T // tm,),
        in_specs=[pl.BlockSpec((tm, D), lambda i: (i, 0)), _const_spec((1, D)), _const_spec((1, D))],
        out_specs=pl.BlockSpec((tm, D), lambda i: (i, 0)),
        out_shape=jax.ShapeDtypeStruct((T, D), F32),
        compiler_params=_cparams("parallel"),
        name="ln_in",
    )(x, g.reshape(1, D), b.reshape(1, D))


def _proj_kernel(dims, h_ref, w_ref, b_ref, kvg_ref,
                 u_ref, rq_ref, rf_ref, ri_ref, rg_ref, aq_ref, c_ref, iq_ref, ikw_ref, gt_ref):
    dc, dr, daq, dl, diq, dg = dims
    x = h_ref[...].astype(BF16)
    off = [0]

    def seg(n):
        o = off[0]
        off[0] = o + n
        return _dot(x, w_ref[:, o:o + n]) + b_ref[:, o:o + n]

    a = seg(dc)
    gate = seg(dc)
    u_ref[...] = (a * _sigmoid(gate)).astype(BF16)
    rq_ref[...] = seg(dr).astype(BF16)
    rf_ref[...] = seg(dr)
    ri_ref[...] = seg(dr).astype(BF16)
    rg_ref[...] = _sigmoid(seg(dr)).astype(BF16)
    scale = dl ** -0.5 * LOG2E
    for j in range(daq // 512):
        aq_ref[:, j * 512:(j + 1) * 512] = (seg(512) * scale).astype(BF16)
    c = seg(dl)
    c = c * lax.rsqrt(jnp.mean(c * c, axis=-1, keepdims=True) + LN_EPS) * kvg_ref[...]
    c_ref[...] = c.astype(BF16)
    iq_ref[...] = seg(diq).astype(BF16)
    ikw_ref[...] = seg(LANES)
    for j in range(dg // 512):
        gt_ref[:, j * 512:(j + 1) * 512] = _sigmoid(seg(512)).astype(BF16)


def _in_projection(h, w, b, kv_g, dims, tm):
    T, D = h.shape
    dc, dr, daq, dl, diq, dg = dims
    n_pad = w.shape[1]
    widths = [(dc, BF16), (dr, BF16), (dr, F32), (dr, BF16), (dr, BF16), (daq, BF16), (dl, BF16),
              (diq, BF16), (LANES, F32), (dg, BF16)]
    return pl.pallas_call(
        functools.partial(_proj_kernel, dims),
        grid=(T // tm,),
        in_specs=[pl.BlockSpec((tm, D), lambda i: (i, 0)), _const_spec((D, n_pad)),
                  _const_spec((1, n_pad)), _const_spec((1, dl))],
        out_specs=[pl.BlockSpec((tm, n), lambda i: (i, 0)) for n, _ in widths],
        out_shape=[jax.ShapeDtypeStruct((T, n), dt) for n, dt in widths],
        compiler_params=_cparams("parallel"),
        name="in_proj",
    )(h, w, b, kv_g.reshape(1, dl))


def _conv_kernel(width, ts, sub, u_ref, halo_ref, dw_ref, db_ref, g_ref, b_ref, o_ref, buf_ref):
    i = pl.program_id(1)
    halo = halo_ref[0].astype(F32)
    buf_ref[0, 0:32, :] = jnp.where(i > 0, halo, 0.0)
    buf_ref[0, 32:32 + ts, :] = u_ref[0].astype(F32)
    for r in range(1, SUBLANES):
        buf_ref[r, 0:ts + 32 - SUBLANES, :] = buf_ref[0, r:r + ts + 32 - SUBLANES, :]
    base = 32 - (width - 1)
    for s in range(ts // sub):
        acc = jnp.zeros((sub, u_ref.shape[2]), F32) + db_ref[...]
        for j in range(width):
            r = (base + j) % SUBLANES
            r0 = base + j - r + s * sub
            acc = acc + buf_ref[r, r0:r0 + sub, :] * dw_ref[j:j + 1, :]
        y = _ln_rows(acc, g_ref[...], b_ref[...])
        o_ref[0, s * sub:(s + 1) * sub, :] = (y * _sigmoid(y)).astype(BF16)


def _conformer_conv(u, dw, db, g, b, ts):
    B, S, C = u.shape
    width = dw.shape[0]
    assert width - 1 <= 32
    sub = min(64, ts)
    hb = ts // 32
    return pl.pallas_call(
        functools.partial(_conv_kernel, width, ts, sub),
        grid=(B, S // ts),
        in_specs=[pl.BlockSpec((1, ts, C), lambda bi, i: (bi, i, 0)),
                  pl.BlockSpec((1, 32, C), lambda bi, i: (bi, jnp.maximum(i * hb - 1, 0), 0)),
                  _const_spec((width, C)), _const_spec((1, C)), _const_spec((1, C)), _const_spec((1, C))],
        out_specs=pl.BlockSpec((1, ts, C), lambda bi, i: (bi, i, 0)),
        out_shape=jax.ShapeDtypeStruct((B, S, C), BF16),
        scratch_shapes=[pltpu.VMEM((SUBLANES, ts + 32, C), F32)],
        compiler_params=_cparams("parallel", "arbitrary"),
        name="conv",
    )(u, u, dw, db.reshape(1, C), g.reshape(1, C), b.reshape(1, C))


def _hgrn_chunk(q, z, v, lb, st, skew_rhs):
    C, Dh = q.shape
    ls = jnp.minimum(z, 0.0) - jnp.log(1.0 + jnp.exp(-jnp.abs(z)))
    a_ = jnp.log(lb)
    b_ = jnp.log1p(-lb) + ls
    lf = jnp.maximum(a_, b_) + jnp.log(1.0 + jnp.exp(-jnp.abs(a_ - b_)))
    kk = (1.0 - lb) * _sigmoid(-z)

    row = lax.broadcasted_iota(I32, (C, C), 0)
    col = lax.broadcasted_iota(I32, (C, C), 1)
    tri = (col <= row).astype(BF16)
    lf_hi = lf.astype(BF16)
    lf_r = lf - lf_hi.astype(F32)
    lf_mid = lf_r.astype(BF16)
    lf_lo = (lf_r - lf_mid.astype(F32)).astype(BF16)
    A = (_dot(tri, lf_hi) + (_dot(tri, lf_mid) + _dot(tri, lf_lo))) * LOG2E

    o = _dot_nt((q * jnp.exp2(A)).astype(BF16), st.astype(BF16))

    rmod = lax.broadcasted_iota(I32, (C, Dh), 0) % RNN_SUB
    parts = []
    for d in range(RNN_SUB):
        if d == 0:
            e = q * kk
        else:
            a_s = pltpu.roll(A, d, axis=0)
            k_s = pltpu.roll(kk, d, axis=0)
            e = jnp.exp2(jnp.where(rmod >= d, A - a_s, NEG_BIG)) * q * k_s
        parts.append(e.astype(BF16))
    z = _dot(jnp.concatenate(parts, axis=1), skew_rhs)
    P = pltpu.roll(z, 0, 1, stride=1, stride_axis=0)[:, :C]

    blocks = [jnp.zeros((RNN_SUB, C), F32)]
    for i in range(1, C // RNN_SUB):
        r0 = i * RNN_SUB
        a_i = A[r0 - 1:r0, :]
        qi = q[r0:r0 + RNN_SUB] * jnp.exp2(A[r0:r0 + RNN_SUB] - a_i)
        ki = (kk[:r0] * jnp.exp2(a_i - A[:r0])).astype(BF16)
        ki = jnp.concatenate([ki, jnp.zeros((C - r0, Dh), BF16)], axis=0)
        blocks.append(_dot_nt(qi.astype(BF16), ki))
    P = P + jnp.concatenate(blocks, axis=0)
    o = o + _dot(P.astype(BF16), v.astype(BF16))

    a_last = A[C - 1:C, :]
    kd = kk * jnp.exp2(a_last - A)
    st_new = st * jnp.exp2(a_last) + _dot_tn(v.astype(BF16), kd.astype(BF16))
    return o, st_new


def _hgrn_kernel(n_chunks, q_ref, z_ref, v_ref, g_ref, lb_ref, ng_ref, skew_ref, o_ref, st_ref):
    @pl.when(pl.program_id(2) == 0)
    def _():
        st_ref[...] = jnp.zeros_like(st_ref)

    C = RNN_CHUNK

    def chunk_body(c, st):
        sl = pl.ds(pl.multiple_of(c * C, C), C)
        o, st = _hgrn_chunk(q_ref[0, sl, :].astype(F32), z_ref[0, sl, :], v_ref[0, sl, :].astype(F32),
                            lb_ref[...], st, skew_ref[...])
        o = o * lax.rsqrt(jnp.mean(o * o, axis=-1, keepdims=True) + LN_EPS) * ng_ref[...]
        o_ref[0, sl, :] = (o * g_ref[0, sl, :].astype(F32)).astype(BF16)
        return st

    st_ref[...] = lax.fori_loop(0, n_chunks, chunk_body, st_ref[...], unroll=min(RNN_UNROLL, n_chunks))


def _hgrn2(rq, rf, ri, rg, lb, norm_g, rows):
    B, S, DR = rq.shape
    H = N_RNN_HEADS
    Dh = DR // H
    blk = pl.BlockSpec((1, rows, Dh), lambda b, h, i: (b, i, h))
    vec = pl.BlockSpec((1, Dh), lambda b, h, i: (0, h))
    assert RNN_CHUNK + RNN_SUB <= LANES
    off = jnp.arange(RNN_SUB * Dh)[:, None] // Dh
    skew_rhs = (jnp.arange(LANES)[None, :] == (LANES - off) % LANES).astype(BF16)
    return pl.pallas_call(
        functools.partial(_hgrn_kernel, rows // RNN_CHUNK),
        grid=(B, H, S // rows),
        in_specs=[blk, blk, blk, blk, vec, vec, _const_spec(skew_rhs.shape)],
        out_specs=blk,
        out_shape=jax.ShapeDtypeStruct((B, S, DR), BF16),
        scratch_shapes=[pltpu.VMEM((Dh, Dh), F32)],
        compiler_params=_cparams("parallel", "parallel", "arbitrary"),
        name="hgrn2",
    )(rq, rf, ri, rg, lb.reshape(1, DR), norm_g.reshape(1, DR), skew_rhs)


def _dsa_kernel(topk, n_heads, aq_ref, c_ref, iq_ref, ik_ref, ikw_q_ref, bias_ref, wuv_ref, o_ref,
                cpad_ref, key_ref, hi_ref, lo_ref, mask_ref, m_ref, l_ref, acc_ref):
    QB, KC = Q_BLOCK, KEY_CHUNK
    PAD = KC - 1
    qi = pl.program_id(1)
    n_chunks = qi // KC + 1
    row = lax.broadcasted_iota(I32, (QB, QB), 0)
    col = lax.broadcasted_iota(I32, (QB, QB), 1)
    dl = c_ref.shape[2]

    @pl.when(qi == 0)
    def _():
        cpad_ref[0:PAD * QB, :] = jnp.zeros((PAD * QB, dl), BF16)
        cpad_ref[PAD * QB:, :] = c_ref[0]

    iq = iq_ref[0]
    q_st = jnp.concatenate([iq[:, h * IDX_HEAD_DIM:(h + 1) * IDX_HEAD_DIM] for h in range(N_IDX_HEADS)], axis=0)
    w_rows = ikw_q_ref[0].T[IDX_HEAD_DIM:IDX_HEAD_DIM + 8, :]
    kpos_minus_q = row - col

    def score_body(ci, carry):
        rows = pl.ds(pl.multiple_of(ci * (KC * QB), KC * QB), KC * QB)
        kj = ik_ref[0, rows, :][:, :IDX_HEAD_DIM].astype(BF16)
        s4 = _dot_nt(kj, q_st)
        for b in range(KC):
            j = ci * KC + b
            sc = jnp.zeros((QB, QB), F32)
            for h in range(N_IDX_HEADS):
                sc = sc + w_rows[h:h + 1, :] * jnp.maximum(s4[b * QB:(b + 1) * QB, h * QB:(h + 1) * QB], 0.0)
            sc = sc + 0.0
            bits = pltpu.bitcast(sc, I32)
            key = bits ^ ((bits >> 31) & 0x7FFFFFFF)
            key = jnp.where(kpos_minus_q <= (qi - j) * QB, key, INT_MIN)
            key_ref[j] = key
            hi_ref[j] = (key >> 16).astype(I16)
            lo_ref[j] = ((key & 0xFFFF) + I16_MIN).astype(I16)
        return carry

    lax.fori_loop(0, n_chunks, score_body, 0)

    def count_rows(acc):
        return jnp.sum(acc.astype(F32), axis=0, keepdims=True)

    def radix16(src_ref, need):
        def bit_body(it, cur):
            cand = cur + lax.shift_left(jnp.int32(1), 15 - it)
            cand16 = jnp.broadcast_to(cand, (QB, QB)).astype(I16)

            def cnt_body(ci, acc):
                for b in range(KC):
                    acc = acc + jnp.where(src_ref[ci * KC + b] >= cand16, jnp.int16(1), jnp.int16(0))
                return acc

            acc = lax.fori_loop(0, n_chunks, cnt_body, jnp.zeros((QB, QB), I16))
            return jnp.where(count_rows(acc) >= need, cand, cur)

        return lax.fori_loop(0, 16, bit_body, jnp.full((1, QB), I16_MIN, I32))

    t_hi = radix16(hi_ref, float(topk))
    t_hi16 = jnp.broadcast_to(t_hi, (QB, QB)).astype(I16)

    def mid_body(ci, acc):
        for b in range(KC):
            j = ci * KC + b
            hi = hi_ref[j]
            acc = acc + jnp.where(hi > t_hi16, jnp.int16(1), jnp.int16(0))
            lo_ref[j] = jnp.where(hi == t_hi16, lo_ref[j], jnp.int16(I16_MIN))
        return acc

    n_hi_gt = count_rows(lax.fori_loop(0, n_chunks, mid_body, jnp.zeros((QB, QB), I16)))
    t_lo = radix16(lo_ref, topk - n_hi_gt)
    thr = t_hi * 65536 + (t_lo - I16_MIN)
    thr = jnp.maximum(thr, INT_MIN + 1)
    thr_b = jnp.broadcast_to(thr, (QB, QB))

    for p in range(PAD):
        mask_ref[p] = jnp.full((QB, QB), NEG_BIG, F32)

    def mask_body(ci, acc):
        for b in range(KC):
            j = ci * KC + b
            ge = key_ref[j] >= thr_b
            mask_ref[PAD + j] = jnp.where(ge, 0.0, NEG_BIG).T
            acc = acc + jnp.where(ge, 1, 0)
        return acc

    n_sel = count_rows(lax.fori_loop(0, n_chunks, mask_body, jnp.zeros((QB, QB), I32)))

    @pl.when(jnp.max(n_sel) > topk)
    def _():
        def gt_body(j, acc):
            return acc + jnp.where(key_ref[j] > thr_b, 1, 0)

        n_gt = count_rows(lax.fori_loop(0, qi + 1, gt_body, jnp.zeros((QB, QB), I32)))
        need_b = jnp.broadcast_to(topk - n_gt, (QB, QB))
        tri = (col <= row).astype(BF16)

        def tie_body(j, carry):
            key = key_ref[j]
            eq = key == thr_b
            cs = _dot(tri, jnp.where(eq, 1.0, 0.0).astype(BF16))
            sel = (key > thr_b) | (eq & (carry + cs <= need_b))
            mask_ref[PAD + j] = jnp.where(sel, 0.0, NEG_BIG).T
            return carry + cs[QB - 1:QB, :]

        lax.fori_loop(0, qi + 1, tie_body, jnp.zeros((1, QB), F32))

    q_all = jnp.concatenate([aq_ref[0, :, h * dl:(h + 1) * dl] for h in range(n_heads)], axis=0)
    n_far = qi // KC

    def logits(start, near):
        cch = cpad_ref[pl.ds(pl.multiple_of(start * QB, QB), KC * QB), :]
        lg = _dot_nt(q_all, cch)
        out = []
        for h in range(n_heads):
            blks = []
            for b in range(KC):
                x = lg[h * QB:(h + 1) * QB, b * QB:(b + 1) * QB] + mask_ref[start + b]
                if near and b == KC - 1:
                    x = x + bias_ref[0, h]
                if near and b == KC - 2:
                    x = x + bias_ref[1, h]
                blks.append(x)
            out.append(blks)
        return cch, out

    def att_step(start, near):
        cch, lg = logits(start, near)
        rows, alphas = [], []
        for h in range(n_heads):
            m_old = m_ref[h]
            mx = lg[h][0]
            for b in range(1, KC):
                mx = jnp.maximum(mx, lg[h][b])
            m_new = jnp.maximum(m_old, jnp.broadcast_to(jnp.max(mx, axis=1, keepdims=True), (QB, QB)))
            alpha = jnp.exp2(m_old - m_new)
            ps = [jnp.exp2(lg[h][b] - m_new) for b in range(KC)]
            s = ps[0]
            for b in range(1, KC):
                s = s + ps[b]
            l_ref[h] = alpha * l_ref[h] + s
            m_ref[h] = m_new
            rows.append(jnp.concatenate([p.astype(BF16) for p in ps], axis=1))
            alphas.append(alpha)
        pv = _dot(jnp.concatenate(rows, axis=0), cch)
        for h in range(n_heads):
            sl = slice(h * QB, (h + 1) * QB)
            acc_ref[sl, :] = alphas[h] * acc_ref[sl, :] + pv[sl, :]

    m_ref[...] = jnp.full(m_ref.shape, NEG_BIG, F32)
    l_ref[...] = jnp.zeros(l_ref.shape, F32)
    acc_ref[...] = jnp.zeros(acc_ref.shape, F32)

    def far_body(m, carry):
        att_step(qi - KC * (m + 1), False)
        return carry

    lax.fori_loop(0, n_far, far_body, 0)
    att_step(qi, True)

    out = jnp.zeros((QB, o_ref.shape[2]), F32)
    for h in range(n_heads):
        oh = acc_ref[h * QB:(h + 1) * QB, :] / jnp.sum(l_ref[h], axis=1, keepdims=True)
        out = out + _dot(oh.astype(BF16), wuv_ref[h])
    o_ref[0] = out.astype(BF16)


def _dsa_attention(aq, c, iq, ikw, bias_near, wuv_pad):
    B, S, DQ = aq.shape
    dl = c.shape[2]
    assert dl == Q_BLOCK
    n_heads = DQ // dl
    d_att = wuv_pad.shape[2]
    topk = min(TOPK_MAX, S // 4)
    QB, KC = Q_BLOCK, KEY_CHUNK
    nkb = S // QB
    assert nkb % KC == 0
    return pl.pallas_call(
        functools.partial(_dsa_kernel, topk, n_heads),
        grid=(B, nkb),
        in_specs=[pl.BlockSpec((1, QB, DQ), lambda b, i: (b, i, 0)),
                  pl.BlockSpec((1, S, dl), lambda b, i: (b, 0, 0)),
                  pl.BlockSpec((1, QB, iq.shape[2]), lambda b, i: (b, i, 0)),
                  pl.BlockSpec((1, S, LANES), lambda b, i: (b, 0, 0)),
                  pl.BlockSpec((1, QB, LANES), lambda b, i: (b, i, 0)),
                  _const_spec(bias_near.shape), _const_spec(wuv_pad.shape)],
        out_specs=pl.BlockSpec((1, QB, d_att), lambda b, i: (b, i, 0)),
        out_shape=jax.ShapeDtypeStruct((B, S, d_att), BF16),
        scratch_shapes=[pltpu.VMEM(((nkb + KC - 1) * QB, dl), BF16),
                        pltpu.VMEM((nkb, QB, QB), I32), pltpu.VMEM((nkb, QB, QB), I16),
                        pltpu.VMEM((nkb, QB, QB), I16), pltpu.VMEM((nkb + KC - 1, QB, QB), F32),
                        pltpu.VMEM((n_heads, QB, QB), F32), pltpu.VMEM((n_heads, QB, QB), F32),
                        pltpu.VMEM((n_heads * QB, dl), F32)],
        compiler_params=_cparams("parallel", "arbitrary"),
        name="dsa",
    )(aq, c, iq, ikw, ikw, bias_near, wuv_pad)


def _t5_bucket(n):
    nf = jnp.maximum(n, 1).astype(F32)
    large = MAX_EXACT + (jnp.log(nf / MAX_EXACT) / math.log(MAX_DISTANCE / MAX_EXACT)
                         * (N_BUCKETS - MAX_EXACT)).astype(I32)
    large = jnp.minimum(large, N_BUCKETS - 1)
    return jnp.where(n < MAX_EXACT, n, large)


def _near_bias(rel_bias):
    QB = Q_BLOCK
    assert QB >= MAX_DISTANCE
    r = jnp.arange(QB)[:, None] - jnp.arange(QB)[None, :]
    far = rel_bias[N_BUCKETS - 1]
    diag = rel_bias[_t5_bucket(jnp.maximum(r, 0))] - far
    prev = rel_bias[_t5_bucket(r + QB)] - far
    return (jnp.stack([diag, prev]).transpose(0, 3, 1, 2) * LOG2E).astype(F32)


def _merge_kernel(alpha, D, h_ref, u_ref, r_ref, a_ref, gt_ref, wc_ref, wr_ref, wa_ref, wo_ref, g_ref, b_ref,
                  o_ref):
    y = gt_ref[:, 0:D].astype(F32) * _dot(u_ref[...], wc_ref[...])
    y = y + gt_ref[:, D:2 * D].astype(F32) * _dot(r_ref[...], wr_ref[...])
    y = y + gt_ref[:, 2 * D:3 * D].astype(F32) * _dot(a_ref[...], wa_ref[...])
    z = alpha * h_ref[...] + _dot(y.astype(BF16), wo_ref[...])
    o_ref[...] = _ln_rows(z, g_ref[...], b_ref[...])


def _merge(h, u, r, a, gt, wc, wr, wa, wo, g, b, alpha, tm):
    T, D = h.shape
    row = lambda n: pl.BlockSpec((tm, n), lambda i: (i, 0))
    return pl.pallas_call(
        functools.partial(_merge_kernel, alpha, D),
        grid=(T // tm,),
        in_specs=[row(D), row(u.shape[1]), row(r.shape[1]), row(a.shape[1]), row(3 * D),
                  _const_spec(wc.shape), _const_spec(wr.shape), _const_spec(wa.shape), _const_spec(wo.shape),
                  _const_spec((1, D)), _const_spec((1, D))],
        out_specs=row(D),
        out_shape=jax.ShapeDtypeStruct((T, D), F32),
        compiler_params=_cparams("parallel"),
        name="merge",
    )(h, u, r, a, gt, wc, wr, wa, wo, g.reshape(1, D), b.reshape(1, D))


def _split_bf16(x):
    hi = x.astype(BF16)
    return hi, (x - hi.astype(F32)).astype(BF16)


def _moe_kernel(alpha, n_exp, de, h_ref, rw_hi_ref, rw_lo_ref, rb_ref, wgu_ref, wdn_ref, wsgu_ref, wsdn_ref,
                g_ref, b_ref, o_ref, xb_ref, gate_ref, acc_ref):
    e = pl.program_id(1)
    lane = lax.broadcasted_iota(I32, gate_ref.shape, 1)

    @pl.when(e == 0)
    def _():
        x = h_ref[...]
        x_hi, x_lo = _split_bf16(x)
        xb_ref[...] = x_hi
        logits = _dot(x_hi, rw_hi_ref[...]) + (_dot(x_lo, rw_hi_ref[...]) + _dot(x_hi, rw_lo_ref[...]))
        scores = _sigmoid(logits)
        sb = scores + rb_ref[...]
        sel = jnp.zeros(sb.shape, jnp.bool_)
        for _ in range(TOPK_EXPERTS):
            m = jnp.max(sb, axis=-1, keepdims=True)
            first = jnp.min(jnp.where(sb == m, lane, n_exp), axis=-1, keepdims=True)
            hit = lane == first
            sel = sel | hit
            sb = jnp.where(hit, -jnp.inf, sb)
        w = jnp.where(sel, scores, 0.0)
        gate_ref[...] = w / jnp.sum(w, axis=-1, keepdims=True) * ROUTE_SCALE
        ds = wsdn_ref.shape[0]
        s = _dot(x_hi, wsgu_ref[...])
        a_s, u_s = s[:, :ds], s[:, ds:]
        acc_ref[...] = _dot((a_s * _sigmoid(a_s) * u_s).astype(BF16), wsdn_ref[...])

    xb = xb_ref[...]
    au = _dot(xb, wgu_ref[0])
    a, u = au[:, :de], au[:, de:]
    ge = jnp.sum(jnp.where(lane == e, gate_ref[...], 0.0), axis=-1, keepdims=True)
    hmid = a * _sigmoid(a) * u * ge
    acc_ref[...] += _dot(hmid.astype(BF16), wdn_ref[0])

    @pl.when(e == n_exp - 1)
    def _():
        o_ref[...] = _ln_rows(alpha * h_ref[...] + acc_ref[...], g_ref[...], b_ref[...])


def _moe(h, rw, rb, wgu, wdn, wsgu, wsdn, g, b, alpha, tm):
    T, D = h.shape
    n_exp, _, de2 = wgu.shape
    de = de2 // 2
    rw_hi, rw_lo = _split_bf16(rw)
    return pl.pallas_call(
        functools.partial(_moe_kernel, alpha, n_exp, de),
        grid=(T // tm, n_exp),
        in_specs=[pl.BlockSpec((tm, D), lambda i, e: (i, 0)),
                  _const_spec(rw.shape), _const_spec(rw.shape), _const_spec((1, n_exp)),
                  pl.BlockSpec((1, D, de2), lambda i, e: (e, 0, 0)),
                  pl.BlockSpec((1, de, D), lambda i, e: (e, 0, 0)),
                  _const_spec(wsgu.shape), _const_spec(wsdn.shape),
                  _const_spec((1, D)), _const_spec((1, D))],
        out_specs=pl.BlockSpec((tm, D), lambda i, e: (i, 0)),
        out_shape=jax.ShapeDtypeStruct((T, D), F32),
        scratch_shapes=[pltpu.VMEM((tm, D), BF16), pltpu.VMEM((tm, n_exp), F32), pltpu.VMEM((tm, D), F32)],
        compiler_params=_cparams("parallel", "arbitrary"),
        name="moe",
    )(h, rw_hi, rw_lo, rb.reshape(1, n_exp), wgu, wdn, wsgu, wsdn, g.reshape(1, D), b.reshape(1, D))


def _pack_in_weights(w, b, dims):
    dc, dr, daq, dl, diq, dg = dims
    aligned = [2 * dc, dr, dr, dr, dr, daq, dl, diq]
    assert all(s % LANES == 0 for s in aligned) and IDX_HEAD_DIM + N_IDX_HEADS <= LANES
    split = sum(aligned) + IDX_HEAD_DIM + N_IDX_HEADS
    assert split + dg == w.shape[1]
    pad = LANES - IDX_HEAD_DIM - N_IDX_HEADS
    wb = w.astype(BF16)
    w_pack = jnp.concatenate([wb[:, :split], jnp.zeros((w.shape[0], pad), BF16), wb[:, split:]], axis=1)
    b_pack = jnp.concatenate([b[:split], jnp.zeros((pad,), b.dtype), b[split:]])
    return w_pack, b_pack.reshape(1, -1).astype(F32)


def kernel(x, ln_in_g, ln_in_b, w_in, b_in, conv_dw, conv_b, conv_ln_g, conv_ln_b, w_conv_proj, hgrn_gamma,
           hgrn_norm_g, w_rnn_proj, kv_norm_g, w_uv, w_att_proj, rel_bias, w_out, ln_mix_g, ln_mix_b,
           router_w, router_b, w_expert_gu, w_expert_dn, w_shared_gu, w_shared_dn, ln_ffn_g, ln_ffn_b):
    B, S, D = x.shape
    T = B * S
    depth = w_in.shape[0]
    dc = conv_dw.shape[2]
    dr = hgrn_gamma.shape[1]
    n_heads, dl, dv = w_uv.shape[1:]
    daq = n_heads * dl
    diq = N_IDX_HEADS * IDX_HEAD_DIM
    dims = (dc, dr, daq, dl, diq, 3 * D)
    alpha = (2 * depth) ** 0.25

    tm = min(512, T)
    lb_all = jnp.cumsum(jax.nn.softmax(hgrn_gamma.astype(F32), axis=0), axis=0)
    lb_all = lb_all - lb_all[0]
    bias_near = _near_bias(rel_bias)

    h = _layer_norm(x.reshape(T, D), ln_in_g, ln_in_b, tm)
    for l in range(depth):
        w_pack, b_pack = _pack_in_weights(w_in[l], b_in[l], dims)
        u, rq, rf, ri, rg, aq, c, iq, ikw, gt = _in_projection(h, w_pack, b_pack, kv_norm_g[l], dims, tm)

        def seq(t):
            return t.reshape(B, S, t.shape[-1])

        y_conv = _conformer_conv(seq(u), conv_dw[l], conv_b[l], conv_ln_g[l], conv_ln_b[l], min(512, S))
        y_rnn = _hgrn2(seq(rq), seq(rf), seq(ri), seq(rg), lb_all[l], hgrn_norm_g[l], min(1024, S))
        wuv_pad = jnp.zeros((n_heads, dl, n_heads * dv), F32)
        for hh in range(n_heads):
            wuv_pad = wuv_pad.at[hh, :, hh * dv:(hh + 1) * dv].set(w_uv[l, hh])
        y_att = _dsa_attention(seq(aq), seq(c), seq(iq), seq(ikw), bias_near, wuv_pad.astype(BF16))
        h = _merge(h, y_conv.reshape(T, -1), y_rnn.reshape(T, -1), y_att.reshape(T, -1), gt,
                   w_conv_proj[l].astype(BF16), w_rnn_proj[l].astype(BF16), w_att_proj[l].astype(BF16),
                   w_out[l].astype(BF16), ln_mix_g[l], ln_mix_b[l], alpha, tm)
        h = _moe(h, router_w[l], router_b[l], w_expert_gu[l].astype(BF16), w_expert_dn[l].astype(BF16),
                 w_shared_gu[l].astype(BF16), w_shared_dn[l].astype(BF16), ln_ffn_g[l], ln_ffn_b[l], alpha,
                 min(1024, T))
    return h.reshape(B, S, D)
```

```python
import functools
import math

import jax
import jax.numpy as jnp
from jax import lax
from jax.experimental import pallas as pl
from jax.experimental.pallas import tpu as pltpu

N_RNN_HEADS = 4
N_IDX_HEADS = 4
IDX_HEAD_DIM = 64
TOPK_MAX = 256
N_BUCKETS = 32
MAX_EXACT = N_BUCKETS // 2
MAX_DISTANCE = 128
TOPK_EXPERTS = 4
ROUTE_SCALE = 2.5
LN_EPS = 1e-5

LANES = 128
SUBLANES = 8
VMEM_LIMIT_BYTES = 56 * 1024 * 1024

Q_BLOCK = 128
KEY_CHUNK = 4
RADIX_VARIANTS = (2, 4, 6)
RNN_CHUNK = 64
RNN_SUB = 16
RNN_UNROLL = 16
NEG_BIG = -1e30
INT_MIN = -2147483648
I16_MIN = -32768
LOG2E = math.log2(math.e)

BF16 = jnp.bfloat16
F32 = jnp.float32
I16 = jnp.int16
I32 = jnp.int32


def _cparams(*sem):
    return pltpu.CompilerParams(dimension_semantics=sem, vmem_limit_bytes=VMEM_LIMIT_BYTES)


def _const_spec(shape):
    nd = len(shape)
    return pl.BlockSpec(shape, lambda *_: (0,) * nd, pipeline_mode=pl.Buffered(1))


def _ln_rows(x, g, b):
    mu = jnp.mean(x, axis=-1, keepdims=True)
    xc = x - mu
    var = jnp.mean(xc * xc, axis=-1, keepdims=True)
    return xc * lax.rsqrt(var + LN_EPS) * g + b


def _sigmoid(x):
    return 1.0 / (1.0 + jnp.exp(-x))


def _dot(a, b):
    return jnp.dot(a, b, preferred_element_type=F32)


def _dot_nt(a, b):
    return lax.dot_general(a, b, (((1,), (1,)), ((), ())), preferred_element_type=F32)


def _dot_tn(a, b):
    return lax.dot_general(a, b, (((0,), (0,)), ((), ())), preferred_element_type=F32)


def _ln_kernel(x_ref, g_ref, b_ref, o_ref):
    o_ref[...] = _ln_rows(x_ref[...], g_ref[...], b_ref[...])


def _layer_norm(x, g, b, tm):
    T, D = x.shape
    return pl.pallas_call(
        _ln_kernel,
        grid=(T // tm,),
        in_specs=[pl.BlockSpec((tm, D), lambda i: (i, 0)), _const_spec((1, D)), _const_spec((1, D))],
        out_specs=pl.BlockSpec((tm, D), lambda i: (i, 0)),
        out_shape=jax.ShapeDtypeStruct((T, D), F32),
        compiler_params=_cparams("parallel"),
        name="ln_in",
    )(x, g.reshape(1, D), b.reshape(1, D))


def _proj_kernel(dims, h_ref, w_ref, b_ref, kvg_ref,
                 u_ref, rq_ref, rf_ref, ri_ref, rg_ref, aq_ref, c_ref, iq_ref, ikw_ref, gt_ref):
    dc, dr, daq, dl, diq, dg = dims
    x = h_ref[...].astype(BF16)
    off = [0]

    def seg(n):
        o = off[0]
        off[0] = o + n
        return _dot(x, w_ref[:, o:o + n]) + b_ref[:, o:o + n]

    a = seg(dc)
    gate = seg(dc)
    u_ref[...] = (a * _sigmoid(gate)).astype(BF16)
    rq_ref[...] = seg(dr).astype(BF16)
    rf_ref[...] = seg(dr)
    ri_ref[...] = seg(dr).astype(BF16)
    rg_ref[...] = _sigmoid(seg(dr)).astype(BF16)
    scale = dl ** -0.5 * LOG2E
    for j in range(daq // 512):
        aq_ref[:, j * 512:(j + 1) * 512] = (seg(512) * scale).astype(BF16)
    c = seg(dl)
    c = c * lax.rsqrt(jnp.mean(c * c, axis=-1, keepdims=True) + LN_EPS) * kvg_ref[...]
    c_ref[...] = c.astype(BF16)
    iq_ref[...] = seg(diq).astype(BF16)
    ikw_ref[...] = seg(LANES)
    for j in range(dg // 512):
        gt_ref[:, j * 512:(j + 1) * 512] = _sigmoid(seg(512)).astype(BF16)


def _in_projection(h, w, b, kv_g, dims, tm):
    T, D = h.shape
    dc, dr, daq, dl, diq, dg = dims
    n_pad = w.shape[1]
    widths = [(dc, BF16), (dr, BF16), (dr, F32), (dr, BF16), (dr, BF16), (daq, BF16), (dl, BF16),
              (diq, BF16), (LANES, F32), (dg, BF16)]
    return pl.pallas_call(
        functools.partial(_proj_kernel, dims),
        grid=(T // tm,),
        in_specs=[pl.BlockSpec((tm, D), lambda i: (i, 0)), _const_spec((D, n_pad)),
                  _const_spec((1, n_pad)), _const_spec((1, dl))],
        out_specs=[pl.BlockSpec((tm, n), lambda i: (i, 0)) for n, _ in widths],
        out_shape=[jax.ShapeDtypeStruct((T, n), dt) for n, dt in widths],
        compiler_params=_cparams("parallel"),
        name="in_proj",
    )(h, w, b, kv_g.reshape(1, dl))


def _conv_kernel(width, ts, sub, u_ref, halo_ref, dw_ref, db_ref, g_ref, b_ref, o_ref, buf_ref):
    i = pl.program_id(1)
    halo = halo_ref[0].astype(F32)
    buf_ref[0, 0:32, :] = jnp.where(i > 0, halo, 0.0)
    buf_ref[0, 32:32 + ts, :] = u_ref[0].astype(F32)
    for r in range(1, SUBLANES):
        buf_ref[r, 0:ts + 32 - SUBLANES, :] = buf_ref[0, r:r + ts + 32 - SUBLANES, :]
    base = 32 - (width - 1)
    for s in range(ts // sub):
        acc = jnp.zeros((sub, u_ref.shape[2]), F32) + db_ref[...]
        for j in range(width):
            r = (base + j) % SUBLANES
            r0 = base + j - r + s * sub
            acc = acc + buf_ref[r, r0:r0 + sub, :] * dw_ref[j:j + 1, :]
        y = _ln_rows(acc, g_ref[...], b_ref[...])
        o_ref[0, s * sub:(s + 1) * sub, :] = (y * _sigmoid(y)).astype(BF16)


def _conformer_conv(u, dw, db, g, b, ts):
    B, S, C = u.shape
    width = dw.shape[0]
    assert width - 1 <= 32
    sub = min(64, ts)
    hb = ts // 32
    return pl.pallas_call(
        functools.partial(_conv_kernel, width, ts, sub),
        grid=(B, S // ts),
        in_specs=[pl.BlockSpec((1, ts, C), lambda bi, i: (bi, i, 0)),
                  pl.BlockSpec((1, 32, C), lambda bi, i: (bi, jnp.maximum(i * hb - 1, 0), 0)),
                  _const_spec((width, C)), _const_spec((1, C)), _const_spec((1, C)), _const_spec((1, C))],
        out_specs=pl.BlockSpec((1, ts, C), lambda bi, i: (bi, i, 0)),
        out_shape=jax.ShapeDtypeStruct((B, S, C), BF16),
        scratch_shapes=[pltpu.VMEM((SUBLANES, ts + 32, C), F32)],
        compiler_params=_cparams("parallel", "arbitrary"),
        name="conv",
    )(u, u, dw, db.reshape(1, C), g.reshape(1, C), b.reshape(1, C))


def _hgrn_chunk(q, z, v, lb, st, skew_rhs):
    C, Dh = q.shape
    ls = jnp.minimum(z, 0.0) - jnp.log(1.0 + jnp.exp(-jnp.abs(z)))
    a_ = jnp.log(lb)
    b_ = jnp.log1p(-lb) + ls
    lf = jnp.maximum(a_, b_) + jnp.log(1.0 + jnp.exp(-jnp.abs(a_ - b_)))
    kk = (1.0 - lb) * _sigmoid(-z)

    row = lax.broadcasted_iota(I32, (C, C), 0)
    col = lax.broadcasted_iota(I32, (C, C), 1)
    tri = (col <= row).astype(BF16)
    lf_hi = lf.astype(BF16)
    lf_r = lf - lf_hi.astype(F32)
    lf_mid = lf_r.astype(BF16)
    lf_lo = (lf_r - lf_mid.astype(F32)).astype(BF16)
    A = (_dot(tri, lf_hi) + (_dot(tri, lf_mid) + _dot(tri, lf_lo))) * LOG2E

    o = _dot_nt((q * jnp.exp2(A)).astype(BF16), st.astype(BF16))

    rmod = lax.broadcasted_iota(I32, (C, Dh), 0) % RNN_SUB
    parts = []
    for d in range(RNN_SUB):
        if d == 0:
            e = q * kk
        else:
            a_s = pltpu.roll(A, d, axis=0)
            k_s = pltpu.roll(kk, d, axis=0)
            e = jnp.exp2(jnp.where(rmod >= d, A - a_s, NEG_BIG)) * q * k_s
        parts.append(e.astype(BF16))
    z = _dot(jnp.concatenate(parts, axis=1), skew_rhs)
    P = pltpu.roll(z, 0, 1, stride=1, stride_axis=0)[:, :C]

    blocks = [jnp.zeros((RNN_SUB, C), F32)]
    for i in range(1, C // RNN_SUB):
        r0 = i * RNN_SUB
        a_i = A[r0 - 1:r0, :]
        qi = q[r0:r0 + RNN_SUB] * jnp.exp2(A[r0:r0 + RNN_SUB] - a_i)
        ki = (kk[:r0] * jnp.exp2(a_i - A[:r0])).astype(BF16)
        ki = jnp.concatenate([ki, jnp.zeros((C - r0, Dh), BF16)], axis=0)
        blocks.append(_dot_nt(qi.astype(BF16), ki))
    P = P + jnp.concatenate(blocks, axis=0)
    o = o + _dot(P.astype(BF16), v.astype(BF16))

    a_last = A[C - 1:C, :]
    kd = kk * jnp.exp2(a_last - A)
    st_new = st * jnp.exp2(a_last) + _dot_tn(v.astype(BF16), kd.astype(BF16))
    return o, st_new


def _hgrn_kernel(n_chunks, q_ref, z_ref, v_ref, g_ref, lb_ref, ng_ref, skew_ref, o_ref, st_ref):
    @pl.when(pl.program_id(2) == 0)
    def _():
        st_ref[...] = jnp.zeros_like(st_ref)

    C = RNN_CHUNK

    def chunk_body(c, st):
        sl = pl.ds(pl.multiple_of(c * C, C), C)
        o, st = _hgrn_chunk(q_ref[0, sl, :].astype(F32), z_ref[0, sl, :], v_ref[0, sl, :].astype(F32),
                            lb_ref[...], st, skew_ref[...])
        o = o * lax.rsqrt(jnp.mean(o * o, axis=-1, keepdims=True) + LN_EPS) * ng_ref[...]
        o_ref[0, sl, :] = (o * g_ref[0, sl, :].astype(F32)).astype(BF16)
        return st

    st_ref[...] = lax.fori_loop(0, n_chunks, chunk_body, st_ref[...], unroll=min(RNN_UNROLL, n_chunks))


def _hgrn2(rq, rf, ri, rg, lb, norm_g, rows):
    B, S, DR = rq.shape
    H = N_RNN_HEADS
    Dh = DR // H
    blk = pl.BlockSpec((1, rows, Dh), lambda b, h, i: (b, i, h))
    vec = pl.BlockSpec((1, Dh), lambda b, h, i: (0, h))
    assert RNN_CHUNK + RNN_SUB <= LANES
    off = jnp.arange(RNN_SUB * Dh)[:, None] // Dh
    skew_rhs = (jnp.arange(LANES)[None, :] == (LANES - off) % LANES).astype(BF16)
    return pl.pallas_call(
        functools.partial(_hgrn_kernel, rows // RNN_CHUNK),
        grid=(B, H, S // rows),
        in_specs=[blk, blk, blk, blk, vec, vec, _const_spec(skew_rhs.shape)],
        out_specs=blk,
        out_shape=jax.ShapeDtypeStruct((B, S, DR), BF16),
        scratch_shapes=[pltpu.VMEM((Dh, Dh), F32)],
        compiler_params=_cparams("parallel", "parallel", "arbitrary"),
        name="hgrn2",
    )(rq, rf, ri, rg, lb.reshape(1, DR), norm_g.reshape(1, DR), skew_rhs)


def _dsa_kernel(topk, n_heads, aq_ref, c_ref, iq_ref, ik_ref, ikw_q_ref, bias_ref, wuv_ref, o_ref,
                cpad_ref, key_ref, hi_ref, lo_ref, mask_ref, m_ref, l_ref, acc_ref, thr_ref):
    QB, KC = Q_BLOCK, KEY_CHUNK
    PAD = KC - 1
    qi = pl.program_id(1)
    n_chunks = qi // KC + 1
    n_total_chunks = key_ref.shape[0] // KC
    row = lax.broadcasted_iota(I32, (QB, QB), 0)
    col = lax.broadcasted_iota(I32, (QB, QB), 1)
    dl = c_ref.shape[2]

    @pl.when(qi == 0)
    def _():
        cpad_ref[0:PAD * QB, :] = jnp.zeros((PAD * QB, dl), BF16)
        cpad_ref[PAD * QB:, :] = c_ref[0]

    iq = iq_ref[0]
    q_st = jnp.concatenate([iq[:, h * IDX_HEAD_DIM:(h + 1) * IDX_HEAD_DIM] for h in range(N_IDX_HEADS)], axis=0)
    w_rows = ikw_q_ref[0].T[IDX_HEAD_DIM:IDX_HEAD_DIM + 8, :]
    kpos_minus_q = row - col

    def score_body(ci, carry):
        rows = pl.ds(pl.multiple_of(ci * (KC * QB), KC * QB), KC * QB)
        kj = ik_ref[0, rows, :][:, :IDX_HEAD_DIM].astype(BF16)
        s4 = _dot_nt(kj, q_st)
        for b in range(KC):
            j = ci * KC + b
            sc = jnp.zeros((QB, QB), F32)
            for h in range(N_IDX_HEADS):
                sc = sc + w_rows[h:h + 1, :] * jnp.maximum(s4[b * QB:(b + 1) * QB, h * QB:(h + 1) * QB], 0.0)
            sc = sc + 0.0
            bits = pltpu.bitcast(sc, I32)
            key = bits ^ ((bits >> 31) & 0x7FFFFFFF)
            key = jnp.where(kpos_minus_q <= (qi - j) * QB, key, INT_MIN)
            key_ref[j] = key
            hi_ref[j] = (key >> 16).astype(I16)
            lo_ref[j] = ((key & 0xFFFF) + I16_MIN).astype(I16)
        return carry

    lax.fori_loop(0, n_chunks, score_body, 0)

    def fill_body(ci, carry):
        for b in range(KC):
            hi_ref[ci * KC + b] = jnp.full((QB, QB), I16_MIN, I16)
            lo_ref[ci * KC + b] = jnp.full((QB, QB), I16_MIN, I16)
        return carry

    lax.fori_loop(n_chunks, n_total_chunks, fill_body, 0)

    def count_rows(acc):
        return jnp.sum(acc.astype(F32), axis=0, keepdims=True)

    def find_threshold(nc):
        def radix16(src_ref, need):
            def bit_body(it, cur):
                cand = cur + lax.shift_left(jnp.int32(1), 15 - it)
                cand16 = jnp.broadcast_to(cand, (QB, QB)).astype(I16)
                acc = jnp.zeros((QB, QB), I16)
                for j in range(nc * KC):
                    acc = acc + jnp.where(src_ref[j] >= cand16, jnp.int16(1), jnp.int16(0))
                return jnp.where(count_rows(acc) >= need, cand, cur)

            return lax.fori_loop(0, 16, bit_body, jnp.full((1, QB), I16_MIN, I32))

        t_hi = radix16(hi_ref, float(topk))
        t_hi16 = jnp.broadcast_to(t_hi, (QB, QB)).astype(I16)
        acc = jnp.zeros((QB, QB), I16)
        for j in range(nc * KC):
            hi = hi_ref[j]
            acc = acc + jnp.where(hi > t_hi16, jnp.int16(1), jnp.int16(0))
            lo_ref[j] = jnp.where(hi == t_hi16, lo_ref[j], jnp.int16(I16_MIN))
        t_lo = radix16(lo_ref, topk - count_rows(acc))
        thr_ref[...] = t_hi * 65536 + (t_lo - I16_MIN)

    prev = 0
    for nc in sorted({min(n, n_total_chunks) for n in RADIX_VARIANTS} | {n_total_chunks}):
        pl.when((n_chunks > prev) & (n_chunks <= nc))(functools.partial(find_threshold, nc))
        prev = nc
    thr = jnp.maximum(thr_ref[...], INT_MIN + 1)
    thr_b = jnp.broadcast_to(thr, (QB, QB))

    for p in range(PAD):
        mask_ref[p] = jnp.full((QB, QB), NEG_BIG, F32)

    def mask_body(ci, acc):
        for b in range(KC):
            j = ci * KC + b
            ge = key_ref[j] >= thr_b
            mask_ref[PAD + j] = jnp.where(ge, 0.0, NEG_BIG).T
            acc = acc + jnp.where(ge, 1, 0)
        return acc

    n_sel = count_rows(lax.fori_loop(0, n_chunks, mask_body, jnp.zeros((QB, QB), I32)))

    @pl.when(jnp.max(n_sel) > topk)
    def _():
        def gt_body(j, acc):
            return acc + jnp.where(key_ref[j] > thr_b, 1, 0)

        n_gt = count_rows(lax.fori_loop(0, qi + 1, gt_body, jnp.zeros((QB, QB), I32)))
        need_b = jnp.broadcast_to(topk - n_gt, (QB, QB))
        tri = (col <= row).astype(BF16)

        def tie_body(j, carry):
            key = key_ref[j]
            eq = key == thr_b
            cs = _dot(tri, jnp.where(eq, 1.0, 0.0).astype(BF16))
            sel = (key > thr_b) | (eq & (carry + cs <= need_b))
            mask_ref[PAD + j] = jnp.where(sel, 0.0, NEG_BIG).T
            return carry + cs[QB - 1:QB, :]

        lax.fori_loop(0, qi + 1, tie_body, jnp.zeros((1, QB), F32))

    q_all = jnp.concatenate([aq_ref[0, :, h * dl:(h + 1) * dl] for h in range(n_heads)], axis=0)
    n_far = qi // KC

    def logits(start, near):
        cch = cpad_ref[pl.ds(pl.multiple_of(start * QB, QB), KC * QB), :]
        lg = _dot_nt(q_all, cch)
        out = []
        for h in range(n_heads):
            blks = []
            for b in range(KC):
                x = lg[h * QB:(h + 1) * QB, b * QB:(b + 1) * QB] + mask_ref[start + b]
                if near and b == KC - 1:
                    x = x + bias_ref[0, h]
                if near and b == KC - 2:
                    x = x + bias_ref[1, h]
                blks.append(x)
            out.append(blks)
        return cch, out

    def att_step(start, near):
        cch, lg = logits(start, near)
        rows, alphas = [], []
        for h in range(n_heads):
            m_old = m_ref[h]
            mx = lg[h][0]
            for b in range(1, KC):
                mx = jnp.maximum(mx, lg[h][b])
            m_new = jnp.maximum(m_old, jnp.broadcast_to(jnp.max(mx, axis=1, keepdims=True), (QB, QB)))
            alpha = jnp.exp2(m_old - m_new)
            ps = [jnp.exp2(lg[h][b] - m_new) for b in range(KC)]
            s = ps[0]
            for b in range(1, KC):
                s = s + ps[b]
            l_ref[h] = alpha * l_ref[h] + s
            m_ref[h] = m_new
            rows.append(jnp.concatenate([p.astype(BF16) for p in ps], axis=1))
            alphas.append(alpha)
        pv = _dot(jnp.concatenate(rows, axis=0), cch)
        for h in range(n_heads):
            sl = slice(h * QB, (h + 1) * QB)
            acc_ref[sl, :] = alphas[h] * acc_ref[sl, :] + pv[sl, :]

    m_ref[...] = jnp.full(m_ref.shape, NEG_BIG, F32)
    l_ref[...] = jnp.zeros(l_ref.shape, F32)
    acc_ref[...] = jnp.zeros(acc_ref.shape, F32)

    def far_body(m, carry):
        att_step(qi - KC * (m + 1), False)
        return carry

    lax.fori_loop(0, n_far, far_body, 0)
    att_step(qi, True)

    out = jnp.zeros((QB, o_ref.shape[2]), F32)
    for h in range(n_heads):
        oh = acc_ref[h * QB:(h + 1) * QB, :] / jnp.sum(l_ref[h], axis=1, keepdims=True)
        out = out + _dot(oh.astype(BF16), wuv_ref[h])
    o_ref[0] = out.astype(BF16)


def _dsa_attention(aq, c, iq, ikw, bias_near, wuv_pad):
    B, S, DQ = aq.shape
    dl = c.shape[2]
    assert dl == Q_BLOCK
    n_heads = DQ // dl
    d_att = wuv_pad.shape[2]
    topk = min(TOPK_MAX, S // 4)
    QB, KC = Q_BLOCK, KEY_CHUNK
    nkb = S // QB
    assert nkb % KC == 0
    return pl.pallas_call(
        functools.partial(_dsa_kernel, topk, n_heads),
        grid=(B, nkb),
        in_specs=[pl.BlockSpec((1, QB, DQ), lambda b, i: (b, i, 0)),
                  pl.BlockSpec((1, S, dl), lambda b, i: (b, 0, 0)),
                  pl.BlockSpec((1, QB, iq.shape[2]), lambda b, i: (b, i, 0)),
                  pl.BlockSpec((1, S, LANES), lambda b, i: (b, 0, 0)),
                  pl.BlockSpec((1, QB, LANES), lambda b, i: (b, i, 0)),
                  _const_spec(bias_near.shape), _const_spec(wuv_pad.shape)],
        out_specs=pl.BlockSpec((1, QB, d_att), lambda b, i: (b, i, 0)),
        out_shape=jax.ShapeDtypeStruct((B, S, d_att), BF16),
        scratch_shapes=[pltpu.VMEM(((nkb + KC - 1) * QB, dl), BF16),
                        pltpu.VMEM((nkb, QB, QB), I32), pltpu.VMEM((nkb, QB, QB), I16),
                        pltpu.VMEM((nkb, QB, QB), I16), pltpu.VMEM((nkb + KC - 1, QB, QB), F32),
                        pltpu.VMEM((n_heads, QB, QB), F32), pltpu.VMEM((n_heads, QB, QB), F32),
                        pltpu.VMEM((n_heads * QB, dl), F32), pltpu.VMEM((1, QB), I32)],
        compiler_params=_cparams("parallel", "arbitrary"),
        name="dsa",
    )(aq, c, iq, ikw, ikw, bias_near, wuv_pad)


def _t5_bucket(n):
    nf = jnp.maximum(n, 1).astype(F32)
    large = MAX_EXACT + (jnp.log(nf / MAX_EXACT) / math.log(MAX_DISTANCE / MAX_EXACT)
                         * (N_BUCKETS - MAX_EXACT)).astype(I32)
    large = jnp.minimum(large, N_BUCKETS - 1)
    return jnp.where(n < MAX_EXACT, n, large)


def _near_bias(rel_bias):
    QB = Q_BLOCK
    assert QB >= MAX_DISTANCE
    r = jnp.arange(QB)[:, None] - jnp.arange(QB)[None, :]
    far = rel_bias[N_BUCKETS - 1]
    diag = rel_bias[_t5_bucket(jnp.maximum(r, 0))] - far
    prev = rel_bias[_t5_bucket(r + QB)] - far
    return (jnp.stack([diag, prev]).transpose(0, 3, 1, 2) * LOG2E).astype(F32)


def _merge_kernel(alpha, D, h_ref, u_ref, r_ref, a_ref, gt_ref, wc_ref, wr_ref, wa_ref, wo_ref, g_ref, b_ref,
                  o_ref):
    y = gt_ref[:, 0:D].astype(F32) * _dot(u_ref[...], wc_ref[...])
    y = y + gt_ref[:, D:2 * D].astype(F32) * _dot(r_ref[...], wr_ref[...])
    y = y + gt_ref[:, 2 * D:3 * D].astype(F32) * _dot(a_ref[...], wa_ref[...])
    z = alpha * h_ref[...] + _dot(y.astype(BF16), wo_ref[...])
    o_ref[...] = _ln_rows(z, g_ref[...], b_ref[...])


def _merge(h, u, r, a, gt, wc, wr, wa, wo, g, b, alpha, tm):
    T, D = h.shape
    row = lambda n: pl.BlockSpec((tm, n), lambda i: (i, 0))
    return pl.pallas_call(
        functools.partial(_merge_kernel, alpha, D),
        grid=(T // tm,),
        in_specs=[row(D), row(u.shape[1]), row(r.shape[1]), row(a.shape[1]), row(3 * D),
                  _const_spec(wc.shape), _const_spec(wr.shape), _const_spec(wa.shape), _const_spec(wo.shape),
                  _const_spec((1, D)), _const_spec((1, D))],
        out_specs=row(D),
        out_shape=jax.ShapeDtypeStruct((T, D), F32),
        compiler_params=_cparams("parallel"),
        name="merge",
    )(h, u, r, a, gt, wc, wr, wa, wo, g.reshape(1, D), b.reshape(1, D))


def _split_bf16(x):
    hi = x.astype(BF16)
    return hi, (x - hi.astype(F32)).astype(BF16)


def _moe_kernel(alpha, n_exp, de, h_ref, rw_hi_ref, rw_lo_ref, rb_ref, wgu_ref, wdn_ref, wsgu_ref, wsdn_ref,
                g_ref, b_ref, o_ref, xb_ref, gate_ref, acc_ref):
    e = pl.program_id(1)
    lane = lax.broadcasted_iota(I32, gate_ref.shape, 1)

    @pl.when(e == 0)
    def _():
        x = h_ref[...]
        x_hi, x_lo = _split_bf16(x)
        xb_ref[...] = x_hi
        logits = _dot(x_hi, rw_hi_ref[...]) + (_dot(x_lo, rw_hi_ref[...]) + _dot(x_hi, rw_lo_ref[...]))
        scores = _sigmoid(logits)
        sb = scores + rb_ref[...]
        sel = jnp.zeros(sb.shape, jnp.bool_)
        for _ in range(TOPK_EXPERTS):
            m = jnp.max(sb, axis=-1, keepdims=True)
            first = jnp.min(jnp.where(sb == m, lane, n_exp), axis=-1, keepdims=True)
            hit = lane == first
            sel = sel | hit
            sb = jnp.where(hit, -jnp.inf, sb)
        w = jnp.where(sel, scores, 0.0)
        gate_ref[...] = w / jnp.sum(w, axis=-1, keepdims=True) * ROUTE_SCALE
        ds = wsdn_ref.shape[0]
        s = _dot(x_hi, wsgu_ref[...])
        a_s, u_s = s[:, :ds], s[:, ds:]
        acc_ref[...] = _dot((a_s * _sigmoid(a_s) * u_s).astype(BF16), wsdn_ref[...])

    xb = xb_ref[...]
    au = _dot(xb, wgu_ref[0])
    a, u = au[:, :de], au[:, de:]
    ge = jnp.sum(jnp.where(lane == e, gate_ref[...], 0.0), axis=-1, keepdims=True)
    hmid = a * _sigmoid(a) * u * ge
    acc_ref[...] += _dot(hmid.astype(BF16), wdn_ref[0])

    @pl.when(e == n_exp - 1)
    def _():
        o_ref[...] = _ln_rows(alpha * h_ref[...] + acc_ref[...], g_ref[...], b_ref[...])


def _moe(h, rw, rb, wgu, wdn, wsgu, wsdn, g, b, alpha, tm):
    T, D = h.shape
    n_exp, _, de2 = wgu.shape
    de = de2 // 2
    rw_hi, rw_lo = _split_bf16(rw)
    return pl.pallas_call(
        functools.partial(_moe_kernel, alpha, n_exp, de),
        grid=(T // tm, n_exp),
        in_specs=[pl.BlockSpec((tm, D), lambda i, e: (i, 0)),
                  _const_spec(rw.shape), _const_spec(rw.shape), _const_spec((1, n_exp)),
                  pl.BlockSpec((1, D, de2), lambda i, e: (e, 0, 0)),
                  pl.BlockSpec((1, de, D), lambda i, e: (e, 0, 0)),
                  _const_spec(wsgu.shape), _const_spec(wsdn.shape),
                  _const_spec((1, D)), _const_spec((1, D))],
        out_specs=pl.BlockSpec((tm, D), lambda i, e: (i, 0)),
        out_shape=jax.ShapeDtypeStruct((T, D), F32),
        scratch_shapes=[pltpu.VMEM((tm, D), BF16), pltpu.VMEM((tm, n_exp), F32), pltpu.VMEM((tm, D), F32)],
        compiler_params=_cparams("parallel", "arbitrary"),
        name="moe",
    )(h, rw_hi, rw_lo, rb.reshape(1, n_exp), wgu, wdn, wsgu, wsdn, g.reshape(1, D), b.reshape(1, D))


def _pack_in_weights(w, b, dims):
    dc, dr, daq, dl, diq, dg = dims
    aligned = [2 * dc, dr, dr, dr, dr, daq, dl, diq]
    assert all(s % LANES == 0 for s in aligned) and IDX_HEAD_DIM + N_IDX_HEADS <= LANES
    split = sum(aligned) + IDX_HEAD_DIM + N_IDX_HEADS
    assert split + dg == w.shape[1]
    pad = LANES - IDX_HEAD_DIM - N_IDX_HEADS
    wb = w.astype(BF16)
    w_pack = jnp.concatenate([wb[:, :split], jnp.zeros((w.shape[0], pad), BF16), wb[:, split:]], axis=1)
    b_pack = jnp.concatenate([b[:split], jnp.zeros((pad,), b.dtype), b[split:]])
    return w_pack, b_pack.reshape(1, -1).astype(F32)


def kernel(x, ln_in_g, ln_in_b, w_in, b_in, conv_dw, conv_b, conv_ln_g, conv_ln_b, w_conv_proj, hgrn_gamma,
           hgrn_norm_g, w_rnn_proj, kv_norm_g, w_uv, w_att_proj, rel_bias, w_out, ln_mix_g, ln_mix_b,
           router_w, router_b, w_expert_gu, w_expert_dn, w_shared_gu, w_shared_dn, ln_ffn_g, ln_ffn_b):
    B, S, D = x.shape
    T = B * S
    depth = w_in.shape[0]
    dc = conv_dw.shape[2]
    dr = hgrn_gamma.shape[1]
    n_heads, dl, dv = w_uv.shape[1:]
    daq = n_heads * dl
    diq = N_IDX_HEADS * IDX_HEAD_DIM
    dims = (dc, dr, daq, dl, diq, 3 * D)
    alpha = (2 * depth) ** 0.25

    tm = min(512, T)
    lb_all = jnp.cumsum(jax.nn.softmax(hgrn_gamma.astype(F32), axis=0), axis=0)
    lb_all = lb_all - lb_all[0]
    bias_near = _near_bias(rel_bias)

    h = _layer_norm(x.reshape(T, D), ln_in_g, ln_in_b, tm)
    for l in range(depth):
        w_pack, b_pack = _pack_in_weights(w_in[l], b_in[l], dims)
        u, rq, rf, ri, rg, aq, c, iq, ikw, gt = _in_projection(h, w_pack, b_pack, kv_norm_g[l], dims, tm)

        def seq(t):
            return t.reshape(B, S, t.shape[-1])

        y_conv = _conformer_conv(seq(u), conv_dw[l], conv_b[l], conv_ln_g[l], conv_ln_b[l], min(512, S))
        y_rnn = _hgrn2(seq(rq), seq(rf), seq(ri), seq(rg), lb_all[l], hgrn_norm_g[l], min(1024, S))
        wuv_pad = jnp.zeros((n_heads, dl, n_heads * dv), F32)
        for hh in range(n_heads):
            wuv_pad = wuv_pad.at[hh, :, hh * dv:(hh + 1) * dv].set(w_uv[l, hh])
        y_att = _dsa_attention(seq(aq), seq(c), seq(iq), seq(ikw), bias_near, wuv_pad.astype(BF16))
        h = _merge(h, y_conv.reshape(T, -1), y_rnn.reshape(T, -1), y_att.reshape(T, -1), gt,
                   w_conv_proj[l].astype(BF16), w_rnn_proj[l].astype(BF16), w_att_proj[l].astype(BF16),
                   w_out[l].astype(BF16), ln_mix_g[l], ln_mix_b[l], alpha, tm)
        h = _moe(h, router_w[l], router_b[l], w_expert_gu[l].astype(BF16), w_expert_dn[l].astype(BF16),
                 w_shared_gu[l].astype(BF16), w_shared_dn[l].astype(BF16), ln_ffn_g[l], ln_ffn_b[l], alpha,
                 min(1024, T))
    return h.reshape(B, S, D)
```

```python
import functools
import math

import jax
import jax.numpy as jnp
from jax import lax
from jax.experimental import pallas as pl
from jax.experimental.pallas import tpu as pltpu

N_RNN_HEADS = 4
N_IDX_HEADS = 4
IDX_HEAD_DIM = 64
TOPK_MAX = 256
N_BUCKETS = 32
MAX_EXACT = N_BUCKETS // 2
MAX_DISTANCE = 128
TOPK_EXPERTS = 4
ROUTE_SCALE = 2.5
LN_EPS = 1e-5

LANES = 128
SUBLANES = 8
VMEM_LIMIT_BYTES = 56 * 1024 * 1024

Q_BLOCK = 128
KEY_CHUNK = 4
RADIX_VARIANTS = (2, 4, 6)
RNN_CHUNK = 64
RNN_SUB = 16
RNN_UNROLL = 16
NEG_BIG = -1e30
INT_MIN = -2147483648
I16_MIN = -32768
LOG2E = math.log2(math.e)

BF16 = jnp.bfloat16
F32 = jnp.float32
I16 = jnp.int16
I32 = jnp.int32


def _cparams(*sem):
    return pltpu.CompilerParams(dimension_semantics=sem, vmem_limit_bytes=VMEM_LIMIT_BYTES)


def _const_spec(shape):
    nd = len(shape)
    return pl.BlockSpec(shape, lambda *_: (0,) * nd, pipeline_mode=pl.Buffered(1))


def _ln_rows(x, g, b):
    mu = jnp.mean(x, axis=-1, keepdims=True)
    xc = x - mu
    var = jnp.mean(xc * xc, axis=-1, keepdims=True)
    return xc * lax.rsqrt(var + LN_EPS) * g + b


def _sigmoid(x):
    return 1.0 / (1.0 + jnp.exp(-x))


def _dot(a, b):
    return jnp.dot(a, b, preferred_element_type=F32)


def _dot_nt(a, b):
    return lax.dot_general(a, b, (((1,), (1,)), ((), ())), preferred_element_type=F32)


def _dot_tn(a, b):
    return lax.dot_general(a, b, (((0,), (0,)), ((), ())), preferred_element_type=F32)


def _ln_kernel(x_ref, g_ref, b_ref, o_ref):
    o_ref[...] = _ln_rows(x_ref[...], g_ref[...], b_ref[...])


def _layer_norm(x, g, b, tm):
    T, D = x.shape
    return pl.pallas_call(
        _ln_kernel,
        grid=(T // tm,),
        in_specs=[pl.BlockSpec((tm, D), lambda i: (i, 0)), _const_spec((1, D)), _const_spec((1, D))],
        out_specs=pl.BlockSpec((tm, D), lambda i: (i, 0)),
        out_shape=jax.ShapeDtypeStruct((T, D), F32),
        compiler_params=_cparams("parallel"),
        name="ln_in",
    )(x, g.reshape(1, D), b.reshape(1, D))


def _proj_kernel(dims, h_ref, w_ref, b_ref, kvg_ref,
                 u_ref, rq_ref, rf_ref, ri_ref, rg_ref, aq_ref, c_ref, iq_ref, ikw_ref, gt_ref):
    dc, dr, daq, dl, diq, dg = dims
    x = h_ref[...].astype(BF16)
    off = [0]

    def seg(n):
        o = off[0]
        off[0] = o + n
        return _dot(x, w_ref[:, o:o + n]) + b_ref[:, o:o + n]

    a = seg(dc)
    gate = seg(dc)
    u_ref[...] = (a * _sigmoid(gate)).astype(BF16)
    rq_ref[...] = seg(dr).astype(BF16)
    rf_ref[...] = seg(dr)
    ri_ref[...] = seg(dr).astype(BF16)
    rg_ref[...] = _sigmoid(seg(dr)).astype(BF16)
    scale = dl ** -0.5 * LOG2E
    for j in range(daq // 512):
        aq_ref[:, j * 512:(j + 1) * 512] = (seg(512) * scale).astype(BF16)
    c = seg(dl)
    c = c * lax.rsqrt(jnp.mean(c * c, axis=-1, keepdims=True) + LN_EPS) * kvg_ref[...]
    c_ref[...] = c.astype(BF16)
    iq_ref[...] = seg(diq).astype(BF16)
    ikw_ref[...] = seg(LANES)
    for j in range(dg // 512):
        gt_ref[:, j * 512:(j + 1) * 512] = _sigmoid(seg(512)).astype(BF16)


def _in_projection(h, w, b, kv_g, dims, tm):
    T, D = h.shape
    dc, dr, daq, dl, diq, dg = dims
    n_pad = w.shape[1]
    widths = [(dc, BF16), (dr, BF16), (dr, F32), (dr, BF16), (dr, BF16), (daq, BF16), (dl, BF16),
              (diq, BF16), (LANES, F32), (dg, BF16)]
    return pl.pallas_call(
        functools.partial(_proj_kernel, dims),
        grid=(T // tm,),
        in_specs=[pl.BlockSpec((tm, D), lambda i: (i, 0)), _const_spec((D, n_pad)),
                  _const_spec((1, n_pad)), _const_spec((1, dl))],
        out_specs=[pl.BlockSpec((tm, n), lambda i: (i, 0)) for n, _ in widths],
        out_shape=[jax.ShapeDtypeStruct((T, n), dt) for n, dt in widths],
        compiler_params=_cparams("parallel"),
        name="in_proj",
    )(h, w, b, kv_g.reshape(1, dl))


def _conv_kernel(width, ts, sub, u_ref, halo_ref, dw_ref, db_ref, g_ref, b_ref, o_ref, buf_ref):
    i = pl.program_id(1)
    halo = halo_ref[0].astype(F32)
    buf_ref[0, 0:32, :] = jnp.where(i > 0, halo, 0.0)
    buf_ref[0, 32:32 + ts, :] = u_ref[0].astype(F32)
    for r in range(1, SUBLANES):
        buf_ref[r, 0:ts + 32 - SUBLANES, :] = buf_ref[0, r:r + ts + 32 - SUBLANES, :]
    base = 32 - (width - 1)
    for s in range(ts // sub):
        acc = jnp.zeros((sub, u_ref.shape[2]), F32) + db_ref[...]
        for j in range(width):
            r = (base + j) % SUBLANES
            r0 = base + j - r + s * sub
            acc = acc + buf_ref[r, r0:r0 + sub, :] * dw_ref[j:j + 1, :]
        y = _ln_rows(acc, g_ref[...], b_ref[...])
        o_ref[0, s * sub:(s + 1) * sub, :] = (y * _sigmoid(y)).astype(BF16)


def _conformer_conv(u, dw, db, g, b, ts):
    B, S, C = u.shape
    width = dw.shape[0]
    assert width - 1 <= 32
    sub = min(64, ts)
    hb = ts // 32
    return pl.pallas_call(
        functools.partial(_conv_kernel, width, ts, sub),
        grid=(B, S // ts),
        in_specs=[pl.BlockSpec((1, ts, C), lambda bi, i: (bi, i, 0)),
                  pl.BlockSpec((1, 32, C), lambda bi, i: (bi, jnp.maximum(i * hb - 1, 0), 0)),
                  _const_spec((width, C)), _const_spec((1, C)), _const_spec((1, C)), _const_spec((1, C))],
        out_specs=pl.BlockSpec((1, ts, C), lambda bi, i: (bi, i, 0)),
        out_shape=jax.ShapeDtypeStruct((B, S, C), BF16),
        scratch_shapes=[pltpu.VMEM((SUBLANES, ts + 32, C), F32)],
        compiler_params=_cparams("parallel", "arbitrary"),
        name="conv",
    )(u, u, dw, db.reshape(1, C), g.reshape(1, C), b.reshape(1, C))


def _hgrn_chunk(q, z, v, lb, st, skew_rhs):
    C, Dh = q.shape
    ls = jnp.minimum(z, 0.0) - jnp.log(1.0 + jnp.exp(-jnp.abs(z)))
    a_ = jnp.log(lb)
    b_ = jnp.log1p(-lb) + ls
    lf = jnp.maximum(a_, b_) + jnp.log(1.0 + jnp.exp(-jnp.abs(a_ - b_)))
    kk = (1.0 - lb) * _sigmoid(-z)

    row = lax.broadcasted_iota(I32, (C, C), 0)
    col = lax.broadcasted_iota(I32, (C, C), 1)
    tri = (col <= row).astype(BF16)
    lf_hi = lf.astype(BF16)
    lf_r = lf - lf_hi.astype(F32)
    lf_mid = lf_r.astype(BF16)
    lf_lo = (lf_r - lf_mid.astype(F32)).astype(BF16)
    A = (_dot(tri, lf_hi) + (_dot(tri, lf_mid) + _dot(tri, lf_lo))) * LOG2E

    o = _dot_nt((q * jnp.exp2(A)).astype(BF16), st.astype(BF16))

    rmod = lax.broadcasted_iota(I32, (C, Dh), 0) % RNN_SUB
    parts = []
    for d in range(RNN_SUB):
        if d == 0:
            e = q * kk
        else:
            a_s = pltpu.roll(A, d, axis=0)
            k_s = pltpu.roll(kk, d, axis=0)
            e = jnp.exp2(jnp.where(rmod >= d, A - a_s, NEG_BIG)) * q * k_s
        parts.append(e.astype(BF16))
    z = _dot(jnp.concatenate(parts, axis=1), skew_rhs)
    P = pltpu.roll(z, 0, 1, stride=1, stride_axis=0)[:, :C]

    blocks = [jnp.zeros((RNN_SUB, C), F32)]
    for i in range(1, C // RNN_SUB):
        r0 = i * RNN_SUB
        a_i = A[r0 - 1:r0, :]
        qi = q[r0:r0 + RNN_SUB] * jnp.exp2(A[r0:r0 + RNN_SUB] - a_i)
        ki = (kk[:r0] * jnp.exp2(a_i - A[:r0])).astype(BF16)
        ki = jnp.concatenate([ki, jnp.zeros((C - r0, Dh), BF16)], axis=0)
        blocks.append(_dot_nt(qi.astype(BF16), ki))
    P = P + jnp.concatenate(blocks, axis=0)
    o = o + _dot(P.astype(BF16), v.astype(BF16))

    a_last = A[C - 1:C, :]
    kd = kk * jnp.exp2(a_last - A)
    st_new = st * jnp.exp2(a_last) + _dot_tn(v.astype(BF16), kd.astype(BF16))
    return o, st_new


def _hgrn_kernel(n_chunks, q_ref, z_ref, v_ref, g_ref, lb_ref, ng_ref, skew_ref, o_ref, st_ref):
    @pl.when(pl.program_id(2) == 0)
    def _():
        st_ref[...] = jnp.zeros_like(st_ref)

    C = RNN_CHUNK

    def chunk_body(c, st):
        sl = pl.ds(pl.multiple_of(c * C, C), C)
        o, st = _hgrn_chunk(q_ref[0, sl, :].astype(F32), z_ref[0, sl, :], v_ref[0, sl, :].astype(F32),
                            lb_ref[...], st, skew_ref[...])
        o = o * lax.rsqrt(jnp.mean(o * o, axis=-1, keepdims=True) + LN_EPS) * ng_ref[...]
        o_ref[0, sl, :] = (o * g_ref[0, sl, :].astype(F32)).astype(BF16)
        return st

    st_ref[...] = lax.fori_loop(0, n_chunks, chunk_body, st_ref[...], unroll=min(RNN_UNROLL, n_chunks))


def _hgrn2(rq, rf, ri, rg, lb, norm_g, rows):
    B, S, DR = rq.shape
    H = N_RNN_HEADS
    Dh = DR // H
    blk = pl.BlockSpec((1, rows, Dh), lambda b, h, i: (b, i, h))
    vec = pl.BlockSpec((1, Dh), lambda b, h, i: (0, h))
    assert RNN_CHUNK + RNN_SUB <= LANES
    off = jnp.arange(RNN_SUB * Dh)[:, None] // Dh
    skew_rhs = (jnp.arange(LANES)[None, :] == (LANES - off) % LANES).astype(BF16)
    return pl.pallas_call(
        functools.partial(_hgrn_kernel, rows // RNN_CHUNK),
        grid=(B, H, S // rows),
        in_specs=[blk, blk, blk, blk, vec, vec, _const_spec(skew_rhs.shape)],
        out_specs=blk,
        out_shape=jax.ShapeDtypeStruct((B, S, DR), BF16),
        scratch_shapes=[pltpu.VMEM((Dh, Dh), F32)],
        compiler_params=_cparams("parallel", "parallel", "arbitrary"),
        name="hgrn2",
    )(rq, rf, ri, rg, lb.reshape(1, DR), norm_g.reshape(1, DR), skew_rhs)


def _dsa_kernel(topk, n_heads, aq_ref, c_ref, iq_ref, ik_ref, ikw_q_ref, bias_ref, wuv_ref, o_ref,
                cpad_ref, key_ref, hi_ref, lo_ref, mask_ref, m_ref, l_ref, acc_ref, thr_ref, need_ref):
    QB, KC = Q_BLOCK, KEY_CHUNK
    PAD = KC - 1
    qi = pl.program_id(1)
    n_chunks = qi // KC + 1
    n_total_chunks = key_ref.shape[0] // KC
    row = lax.broadcasted_iota(I32, (QB, QB), 0)
    col = lax.broadcasted_iota(I32, (QB, QB), 1)
    dl = c_ref.shape[2]

    @pl.when(qi == 0)
    def _():
        cpad_ref[0:PAD * QB, :] = jnp.zeros((PAD * QB, dl), BF16)
        cpad_ref[PAD * QB:, :] = c_ref[0]

    iq = iq_ref[0]
    q_st = jnp.concatenate([iq[:, h * IDX_HEAD_DIM:(h + 1) * IDX_HEAD_DIM] for h in range(N_IDX_HEADS)], axis=0)
    w_rows = ikw_q_ref[0].T[IDX_HEAD_DIM:IDX_HEAD_DIM + 8, :]
    kpos_minus_q = row - col

    def score_body(ci, carry):
        rows = pl.ds(pl.multiple_of(ci * (KC * QB), KC * QB), KC * QB)
        kj = ik_ref[0, rows, :][:, :IDX_HEAD_DIM].astype(BF16)
        s4 = _dot_nt(kj, q_st)
        for b in range(KC):
            j = ci * KC + b
            sc = jnp.zeros((QB, QB), F32)
            for h in range(N_IDX_HEADS):
                sc = sc + w_rows[h:h + 1, :] * jnp.maximum(s4[b * QB:(b + 1) * QB, h * QB:(h + 1) * QB], 0.0)
            sc = sc + 0.0
            bits = pltpu.bitcast(sc, I32)
            key = bits ^ ((bits >> 31) & 0x7FFFFFFF)
            key = jnp.where(kpos_minus_q <= (qi - j) * QB, key, INT_MIN)
            key_ref[j] = key
            hi_ref[j] = (key >> 16).astype(I16)
            lo_ref[j] = ((key & 0xFFFF) + I16_MIN).astype(I16)
        return carry

    lax.fori_loop(0, n_chunks, score_body, 0)

    def fill_body(ci, carry):
        for b in range(KC):
            hi_ref[ci * KC + b] = jnp.full((QB, QB), I16_MIN, I16)
            lo_ref[ci * KC + b] = jnp.full((QB, QB), I16_MIN, I16)
        return carry

    lax.fori_loop(n_chunks, n_total_chunks, fill_body, 0)

    def count_rows(acc):
        return jnp.sum(acc.astype(F32), axis=0, keepdims=True)

    def find_threshold(nc):
        def radix16(src_ref, need):
            def bit_body(it, cur):
                cand = cur + lax.shift_left(jnp.int32(1), 15 - it)
                cand16 = jnp.broadcast_to(cand, (QB, QB)).astype(I16)
                acc = jnp.zeros((QB, QB), I16)
                for j in range(nc * KC):
                    acc = acc + jnp.where(src_ref[j] >= cand16, jnp.int16(1), jnp.int16(0))
                return jnp.where(count_rows(acc) >= need, cand, cur)

            return lax.fori_loop(0, 16, bit_body, jnp.full((1, QB), I16_MIN, I32))

        t_hi = radix16(hi_ref, float(topk))
        t_hi16 = jnp.broadcast_to(t_hi, (QB, QB)).astype(I16)
        acc = jnp.zeros((QB, QB), I16)
        for j in range(nc * KC):
            hi = hi_ref[j]
            acc = acc + jnp.where(hi > t_hi16, jnp.int16(1), jnp.int16(0))
            lo_ref[j] = jnp.where(hi == t_hi16, lo_ref[j], jnp.int16(I16_MIN))
        n_hi_gt = count_rows(acc)
        t_lo = radix16(lo_ref, topk - n_hi_gt)
        t_lo16 = jnp.broadcast_to(t_lo, (QB, QB)).astype(I16)
        acc = jnp.zeros((QB, QB), I16)
        for j in range(nc * KC):
            acc = acc + jnp.where(lo_ref[j] > t_lo16, jnp.int16(1), jnp.int16(0))
        thr_ref[...] = t_hi * 65536 + (t_lo - I16_MIN)
        need_ref[...] = topk - (n_hi_gt + count_rows(acc))

    prev = 0
    for nc in sorted({min(n, n_total_chunks) for n in RADIX_VARIANTS} | {n_total_chunks}):
        pl.when((n_chunks > prev) & (n_chunks <= nc))(functools.partial(find_threshold, nc))
        prev = nc
    thr = jnp.maximum(thr_ref[...], INT_MIN + 1)
    thr_b = jnp.broadcast_to(thr, (QB, QB))
    need_b = jnp.broadcast_to(need_ref[...], (QB, QB))

    for p in range(PAD):
        mask_ref[p] = jnp.full((QB, QB), NEG_BIG, F32)
    tri = (col <= row).astype(BF16)

    def mask_body(ci, taken):
        keys = [key_ref[ci * KC + b] for b in range(KC)]
        eqs = [k == thr_b for k in keys]
        cnts = [_dot(tri, jnp.where(e, 1.0, 0.0).astype(BF16)) for e in eqs]
        for b in range(KC):
            tie = jnp.where(taken + cnts[b] <= need_b, 0.0, NEG_BIG)
            m = jnp.where(keys[b] > thr_b, 0.0, jnp.where(eqs[b], tie, NEG_BIG))
            mask_ref[PAD + ci * KC + b] = m.T
            taken = taken + cnts[b][QB - 1:QB, :]
        return taken

    lax.fori_loop(0, n_chunks, mask_body, jnp.zeros((1, QB), F32))

    q_all = jnp.concatenate([aq_ref[0, :, h * dl:(h + 1) * dl] for h in range(n_heads)], axis=0)
    n_far = qi // KC

    def logits(start, near):
        cch = cpad_ref[pl.ds(pl.multiple_of(start * QB, QB), KC * QB), :]
        lg = _dot_nt(q_all, cch)
        out = []
        for h in range(n_heads):
            blks = []
            for b in range(KC):
                x = lg[h * QB:(h + 1) * QB, b * QB:(b + 1) * QB] + mask_ref[start + b]
                if near and b == KC - 1:
                    x = x + bias_ref[0, h]
                if near and b == KC - 2:
                    x = x + bias_ref[1, h]
                blks.append(x)
            out.append(blks)
        return cch, out

    def att_step(start, near):
        cch, lg = logits(start, near)
        rows, alphas = [], []
        for h in range(n_heads):
            m_old = m_ref[h]
            mx = lg[h][0]
            for b in range(1, KC):
                mx = jnp.maximum(mx, lg[h][b])
            m_new = jnp.maximum(m_old, jnp.broadcast_to(jnp.max(mx, axis=1, keepdims=True), (QB, QB)))
            alpha = jnp.exp2(m_old - m_new)
            ps = [jnp.exp2(lg[h][b] - m_new) for b in range(KC)]
            s = ps[0]
            for b in range(1, KC):
                s = s + ps[b]
            l_ref[h] = alpha * l_ref[h] + s
            m_ref[h] = m_new
            rows.append(jnp.concatenate([p.astype(BF16) for p in ps], axis=1))
            alphas.append(alpha)
        pv = _dot(jnp.concatenate(rows, axis=0), cch)
        for h in range(n_heads):
            sl = slice(h * QB, (h + 1) * QB)
            acc_ref[sl, :] = alphas[h] * acc_ref[sl, :] + pv[sl, :]

    m_ref[...] = jnp.full(m_ref.shape, NEG_BIG, F32)
    l_ref[...] = jnp.zeros(l_ref.shape, F32)
    acc_ref[...] = jnp.zeros(acc_ref.shape, F32)

    def far_body(m, carry):
        att_step(qi - KC * (m + 1), False)
        return carry

    lax.fori_loop(0, n_far, far_body, 0)
    att_step(qi, True)

    out = jnp.zeros((QB, o_ref.shape[2]), F32)
    for h in range(n_heads):
        oh = acc_ref[h * QB:(h + 1) * QB, :] / jnp.sum(l_ref[h], axis=1, keepdims=True)
        out = out + _dot(oh.astype(BF16), wuv_ref[h])
    o_ref[0] = out.astype(BF16)


def _dsa_attention(aq, c, iq, ikw, bias_near, wuv_pad):
    B, S, DQ = aq.shape
    dl = c.shape[2]
    assert dl == Q_BLOCK
    n_heads = DQ // dl
    d_att = wuv_pad.shape[2]
    topk = min(TOPK_MAX, S // 4)
    QB, KC = Q_BLOCK, KEY_CHUNK
    nkb = S // QB
    assert nkb % KC == 0
    return pl.pallas_call(
        functools.partial(_dsa_kernel, topk, n_heads),
        grid=(B, nkb),
        in_specs=[pl.BlockSpec((1, QB, DQ), lambda b, i: (b, i, 0)),
                  pl.BlockSpec((1, S, dl), lambda b, i: (b, 0, 0)),
                  pl.BlockSpec((1, QB, iq.shape[2]), lambda b, i: (b, i, 0)),
                  pl.BlockSpec((1, S, LANES), lambda b, i: (b, 0, 0)),
                  pl.BlockSpec((1, QB, LANES), lambda b, i: (b, i, 0)),
                  _const_spec(bias_near.shape), _const_spec(wuv_pad.shape)],
        out_specs=pl.BlockSpec((1, QB, d_att), lambda b, i: (b, i, 0)),
        out_shape=jax.ShapeDtypeStruct((B, S, d_att), BF16),
        scratch_shapes=[pltpu.VMEM(((nkb + KC - 1) * QB, dl), BF16),
                        pltpu.VMEM((nkb, QB, QB), I32), pltpu.VMEM((nkb, QB, QB), I16),
                        pltpu.VMEM((nkb, QB, QB), I16), pltpu.VMEM((nkb + KC - 1, QB, QB), F32),
                        pltpu.VMEM((n_heads, QB, QB), F32), pltpu.VMEM((n_heads, QB, QB), F32),
                        pltpu.VMEM((n_heads * QB, dl), F32), pltpu.VMEM((1, QB), I32),
                        pltpu.VMEM((1, QB), F32)],
        compiler_params=_cparams("parallel", "arbitrary"),
        name="dsa",
    )(aq, c, iq, ikw, ikw, bias_near, wuv_pad)


def _t5_bucket(n):
    nf = jnp.maximum(n, 1).astype(F32)
    large = MAX_EXACT + (jnp.log(nf / MAX_EXACT) / math.log(MAX_DISTANCE / MAX_EXACT)
                         * (N_BUCKETS - MAX_EXACT)).astype(I32)
    large = jnp.minimum(large, N_BUCKETS - 1)
    return jnp.where(n < MAX_EXACT, n, large)


def _near_bias(rel_bias):
    QB = Q_BLOCK
    assert QB >= MAX_DISTANCE
    r = jnp.arange(QB)[:, None] - jnp.arange(QB)[None, :]
    far = rel_bias[N_BUCKETS - 1]
    diag = rel_bias[_t5_bucket(jnp.maximum(r, 0))] - far
    prev = rel_bias[_t5_bucket(r + QB)] - far
    return (jnp.stack([diag, prev]).transpose(0, 3, 1, 2) * LOG2E).astype(F32)


def _merge_kernel(alpha, D, h_ref, u_ref, r_ref, a_ref, gt_ref, wc_ref, wr_ref, wa_ref, wo_ref, g_ref, b_ref,
                  o_ref):
    y = gt_ref[:, 0:D].astype(F32) * _dot(u_ref[...], wc_ref[...])
    y = y + gt_ref[:, D:2 * D].astype(F32) * _dot(r_ref[...], wr_ref[...])
    y = y + gt_ref[:, 2 * D:3 * D].astype(F32) * _dot(a_ref[...], wa_ref[...])
    z = alpha * h_ref[...] + _dot(y.astype(BF16), wo_ref[...])
    o_ref[...] = _ln_rows(z, g_ref[...], b_ref[...])


def _merge(h, u, r, a, gt, wc, wr, wa, wo, g, b, alpha, tm):
    T, D = h.shape
    row = lambda n: pl.BlockSpec((tm, n), lambda i: (i, 0))
    return pl.pallas_call(
        functools.partial(_merge_kernel, alpha, D),
        grid=(T // tm,),
        in_specs=[row(D), row(u.shape[1]), row(r.shape[1]), row(a.shape[1]), row(3 * D),
                  _const_spec(wc.shape), _const_spec(wr.shape), _const_spec(wa.shape), _const_spec(wo.shape),
                  _const_spec((1, D)), _const_spec((1, D))],
        out_specs=row(D),
        out_shape=jax.ShapeDtypeStruct((T, D), F32),
        compiler_params=_cparams("parallel"),
        name="merge",
    )(h, u, r, a, gt, wc, wr, wa, wo, g.reshape(1, D), b.reshape(1, D))


def _split_bf16(x):
    hi = x.astype(BF16)
    return hi, (x - hi.astype(F32)).astype(BF16)


def _moe_kernel(alpha, n_exp, de, h_ref, rw_hi_ref, rw_lo_ref, rb_ref, wgu_ref, wdn_ref, wsgu_ref, wsdn_ref,
                g_ref, b_ref, o_ref, xb_ref, gate_ref, acc_ref):
    e = pl.program_id(1)
    lane = lax.broadcasted_iota(I32, gate_ref.shape, 1)

    @pl.when(e == 0)
    def _():
        x = h_ref[...]
        x_hi, x_lo = _split_bf16(x)
        xb_ref[...] = x_hi
        logits = _dot(x_hi, rw_hi_ref[...]) + (_dot(x_lo, rw_hi_ref[...]) + _dot(x_hi, rw_lo_ref[...]))
        scores = _sigmoid(logits)
        sb = scores + rb_ref[...]
        sel = jnp.zeros(sb.shape, jnp.bool_)
        for _ in range(TOPK_EXPERTS):
            m = jnp.max(sb, axis=-1, keepdims=True)
            first = jnp.min(jnp.where(sb == m, lane, n_exp), axis=-1, keepdims=True)
            hit = lane == first
            sel = sel | hit
            sb = jnp.where(hit, -jnp.inf, sb)
        w = jnp.where(sel, scores, 0.0)
        gate_ref[...] = w / jnp.sum(w, axis=-1, keepdims=True) * ROUTE_SCALE
        ds = wsdn_ref.shape[0]
        s = _dot(x_hi, wsgu_ref[...])
        a_s, u_s = s[:, :ds], s[:, ds:]
        acc_ref[...] = _dot((a_s * _sigmoid(a_s) * u_s).astype(BF16), wsdn_ref[...])

    xb = xb_ref[...]
    au = _dot(xb, wgu_ref[0])
    a, u = au[:, :de], au[:, de:]
    ge = jnp.sum(jnp.where(lane == e, gate_ref[...], 0.0), axis=-1, keepdims=True)
    hmid = a * _sigmoid(a) * u * ge
    acc_ref[...] += _dot(hmid.astype(BF16), wdn_ref[0])

    @pl.when(e == n_exp - 1)
    def _():
        o_ref[...] = _ln_rows(alpha * h_ref[...] + acc_ref[...], g_ref[...], b_ref[...])


def _moe(h, rw, rb, wgu, wdn, wsgu, wsdn, g, b, alpha, tm):
    T, D = h.shape
    n_exp, _, de2 = wgu.shape
    de = de2 // 2
    rw_hi, rw_lo = _split_bf16(rw)
    return pl.pallas_call(
        functools.partial(_moe_kernel, alpha, n_exp, de),
        grid=(T // tm, n_exp),
        in_specs=[pl.BlockSpec((tm, D), lambda i, e: (i, 0)),
                  _const_spec(rw.shape), _const_spec(rw.shape), _const_spec((1, n_exp)),
                  pl.BlockSpec((1, D, de2), lambda i, e: (e, 0, 0)),
                  pl.BlockSpec((1, de, D), lambda i, e: (e, 0, 0)),
                  _const_spec(wsgu.shape), _const_spec(wsdn.shape),
                  _const_spec((1, D)), _const_spec((1, D))],
        out_specs=pl.BlockSpec((tm, D), lambda i, e: (i, 0)),
        out_shape=jax.ShapeDtypeStruct((T, D), F32),
        scratch_shapes=[pltpu.VMEM((tm, D), BF16), pltpu.VMEM((tm, n_exp), F32), pltpu.VMEM((tm, D), F32)],
        compiler_params=_cparams("parallel", "arbitrary"),
        name="moe",
    )(h, rw_hi, rw_lo, rb.reshape(1, n_exp), wgu, wdn, wsgu, wsdn, g.reshape(1, D), b.reshape(1, D))


def _pack_in_weights(w, b, dims):
    dc, dr, daq, dl, diq, dg = dims
    aligned = [2 * dc, dr, dr, dr, dr, daq, dl, diq]
    assert all(s % LANES == 0 for s in aligned) and IDX_HEAD_DIM + N_IDX_HEADS <= LANES
    split = sum(aligned) + IDX_HEAD_DIM + N_IDX_HEADS
    assert split + dg == w.shape[1]
    pad = LANES - IDX_HEAD_DIM - N_IDX_HEADS
    wb = w.astype(BF16)
    w_pack = jnp.concatenate([wb[:, :split], jnp.zeros((w.shape[0], pad), BF16), wb[:, split:]], axis=1)
    b_pack = jnp.concatenate([b[:split], jnp.zeros((pad,), b.dtype), b[split:]])
    return w_pack, b_pack.reshape(1, -1).astype(F32)


def kernel(x, ln_in_g, ln_in_b, w_in, b_in, conv_dw, conv_b, conv_ln_g, conv_ln_b, w_conv_proj, hgrn_gamma,
           hgrn_norm_g, w_rnn_proj, kv_norm_g, w_uv, w_att_proj, rel_bias, w_out, ln_mix_g, ln_mix_b,
           router_w, router_b, w_expert_gu, w_expert_dn, w_shared_gu, w_shared_dn, ln_ffn_g, ln_ffn_b):
    B, S, D = x.shape
    T = B * S
    depth = w_in.shape[0]
    dc = conv_dw.shape[2]
    dr = hgrn_gamma.shape[1]
    n_heads, dl, dv = w_uv.shape[1:]
    daq = n_heads * dl
    diq = N_IDX_HEADS * IDX_HEAD_DIM
    dims = (dc, dr, daq, dl, diq, 3 * D)
    alpha = (2 * depth) ** 0.25

    tm = min(512, T)
    lb_all = jnp.cumsum(jax.nn.softmax(hgrn_gamma.astype(F32), axis=0), axis=0)
    lb_all = lb_all - lb_all[0]
    bias_near = _near_bias(rel_bias)

    h = _layer_norm(x.reshape(T, D), ln_in_g, ln_in_b, tm)
    for l in range(depth):
        w_pack, b_pack = _pack_in_weights(w_in[l], b_in[l], dims)
        u, rq, rf, ri, rg, aq, c, iq, ikw, gt = _in_projection(h, w_pack, b_pack, kv_norm_g[l], dims, tm)

        def seq(t):
            return t.reshape(B, S, t.shape[-1])

        y_conv = _conformer_conv(seq(u), conv_dw[l], conv_b[l], conv_ln_g[l], conv_ln_b[l], min(512, S))
        y_rnn = _hgrn2(seq(rq), seq(rf), seq(ri), seq(rg), lb_all[l], hgrn_norm_g[l], min(1024, S))
        wuv_pad = jnp.zeros((n_heads, dl, n_heads * dv), F32)
        for hh in range(n_heads):
            wuv_pad = wuv_pad.at[hh, :, hh * dv:(hh + 1) * dv].set(w_uv[l, hh])
        y_att = _dsa_attention(seq(aq), seq(c), seq(iq), seq(ikw), bias_near, wuv_pad.astype(BF16))
        h = _merge(h, y_conv.reshape(T, -1), y_rnn.reshape(T, -1), y_att.reshape(T, -1), gt,
                   w_conv_proj[l].astype(BF16), w_rnn_proj[l].astype(BF16), w_att_proj[l].astype(BF16),
                   w_out[l].astype(BF16), ln_mix_g[l], ln_mix_b[l], alpha, tm)
        h = _moe(h, router_w[l], router_b[l], w_expert_gu[l].astype(BF16), w_expert_dn[l].astype(BF16),
                 w_shared_gu[l].astype(BF16), w_shared_dn[l].astype(BF16), ln_ffn_g[l], ln_ffn_b[l], alpha,
                 min(1024, T))
    return h.reshape(B, S, D)
```

```python
import functools
import math

import jax
import jax.numpy as jnp
from jax import lax
from jax.experimental import pallas as pl
from jax.experimental.pallas import tpu as pltpu

N_RNN_HEADS = 4
N_IDX_HEADS = 4
IDX_HEAD_DIM = 64
TOPK_MAX = 256
N_BUCKETS = 32
MAX_EXACT = N_BUCKETS // 2
MAX_DISTANCE = 128
TOPK_EXPERTS = 4
ROUTE_SCALE = 2.5
LN_EPS = 1e-5

LANES = 128
SUBLANES = 8
VMEM_LIMIT_BYTES = 56 * 1024 * 1024

Q_BLOCK = 128
KEY_CHUNK = 4
RADIX_VARIANTS = (2, 4, 6)
HEAD_GROUP = 8
MOE_EXPERT_GROUP = 4
RNN_CHUNK = 64
RNN_SUB = 16
RNN_UNROLL = 16
NEG_BIG = -1e30
INT_MIN = -2147483648
DIGIT_BITS = 16
N_DIGITS = 32 // DIGIT_BITS
DIGIT = jnp.int16
DIGIT_MIN = -(1 << (DIGIT_BITS - 1))
assert N_DIGITS == 2
LOG2E = math.log2(math.e)

BF16 = jnp.bfloat16
F32 = jnp.float32
I16 = jnp.int16
I32 = jnp.int32


def _cparams(*sem):
    return pltpu.CompilerParams(dimension_semantics=sem, vmem_limit_bytes=VMEM_LIMIT_BYTES)


def _const_spec(shape):
    nd = len(shape)
    return pl.BlockSpec(shape, lambda *_: (0,) * nd, pipeline_mode=pl.Buffered(1))


def _ln_rows(x, g, b):
    mu = jnp.mean(x, axis=-1, keepdims=True)
    xc = x - mu
    var = jnp.mean(xc * xc, axis=-1, keepdims=True)
    return xc * lax.rsqrt(var + LN_EPS) * g + b


def _sigmoid(x):
    return 1.0 / (1.0 + jnp.exp(-x))


def _dot(a, b):
    return jnp.dot(a, b, preferred_element_type=F32)


def _dot_nt(a, b):
    return lax.dot_general(a, b, (((1,), (1,)), ((), ())), preferred_element_type=F32)


def _dot_tn(a, b):
    return lax.dot_general(a, b, (((0,), (0,)), ((), ())), preferred_element_type=F32)


def _ln_kernel(x_ref, g_ref, b_ref, o_ref):
    o_ref[...] = _ln_rows(x_ref[...], g_ref[...], b_ref[...])


def _layer_norm(x, g, b, tm):
    T, D = x.shape
    return pl.pallas_call(
        _ln_kernel,
        grid=(T // tm,),
        in_specs=[pl.BlockSpec((tm, D), lambda i: (i, 0)), _const_spec((1, D)), _const_spec((1, D))],
        out_specs=pl.BlockSpec((tm, D), lambda i: (i, 0)),
        out_shape=jax.ShapeDtypeStruct((T, D), F32),
        compiler_params=_cparams("parallel"),
        name="ln_in",
    )(x, g.reshape(1, D), b.reshape(1, D))


def _proj_kernel(dims, h_ref, w_ref, b_ref, kvg_ref,
                 u_ref, rq_ref, rf_ref, ri_ref, rg_ref, aq_ref, c_ref, iq_ref, ikw_ref, gt_ref):
    dc, dr, daq, dl, diq, dg = dims
    x = h_ref[...].astype(BF16)
    off = [0]

    def seg(n):
        o = off[0]
        off[0] = o + n
        return _dot(x, w_ref[:, o:o + n]) + b_ref[:, o:o + n]

    a = seg(dc)
    gate = seg(dc)
    u_ref[...] = (a * _sigmoid(gate)).astype(BF16)
    rq_ref[...] = seg(dr).astype(BF16)
    rf_ref[...] = seg(dr)
    ri_ref[...] = seg(dr).astype(BF16)
    rg_ref[...] = _sigmoid(seg(dr)).astype(BF16)
    scale = dl ** -0.5 * LOG2E
    for j in range(daq // 512):
        aq_ref[:, j * 512:(j + 1) * 512] = (seg(512) * scale).astype(BF16)
    c = seg(dl)
    c = c * lax.rsqrt(jnp.mean(c * c, axis=-1, keepdims=True) + LN_EPS) * kvg_ref[...]
    c_ref[...] = c.astype(BF16)
    iq_ref[...] = seg(diq).astype(BF16)
    ikw_ref[...] = seg(LANES)
    for j in range(dg // 512):
        gt_ref[:, j * 512:(j + 1) * 512] = _sigmoid(seg(512)).astype(BF16)


def _in_projection(h, w, b, kv_g, dims, tm):
    T, D = h.shape
    dc, dr, daq, dl, diq, dg = dims
    n_pad = w.shape[1]
    widths = [(dc, BF16), (dr, BF16), (dr, F32), (dr, BF16), (dr, BF16), (daq, BF16), (dl, BF16),
              (diq, BF16), (LANES, F32), (dg, BF16)]
    return pl.pallas_call(
        functools.partial(_proj_kernel, dims),
        grid=(T // tm,),
        in_specs=[pl.BlockSpec((tm, D), lambda i: (i, 0)), _const_spec((D, n_pad)),
                  _const_spec((1, n_pad)), _const_spec((1, dl))],
        out_specs=[pl.BlockSpec((tm, n), lambda i: (i, 0)) for n, _ in widths],
        out_shape=[jax.ShapeDtypeStruct((T, n), dt) for n, dt in widths],
        compiler_params=_cparams("parallel"),
        name="in_proj",
    )(h, w, b, kv_g.reshape(1, dl))


def _conv_kernel(width, ts, sub, u_ref, halo_ref, dw_ref, db_ref, g_ref, b_ref, o_ref, buf_ref):
    i = pl.program_id(1)
    halo = halo_ref[0].astype(F32)
    buf_ref[0, 0:32, :] = jnp.where(i > 0, halo, 0.0)
    buf_ref[0, 32:32 + ts, :] = u_ref[0].astype(F32)
    for r in range(1, SUBLANES):
        buf_ref[r, 0:ts + 32 - SUBLANES, :] = buf_ref[0, r:r + ts + 32 - SUBLANES, :]
    base = 32 - (width - 1)
    for s in range(ts // sub):
        acc = jnp.zeros((sub, u_ref.shape[2]), F32) + db_ref[...]
        for j in range(width):
            r = (base + j) % SUBLANES
            r0 = base + j - r + s * sub
            acc = acc + buf_ref[r, r0:r0 + sub, :] * dw_ref[j:j + 1, :]
        y = _ln_rows(acc, g_ref[...], b_ref[...])
        o_ref[0, s * sub:(s + 1) * sub, :] = (y * _sigmoid(y)).astype(BF16)


def _conformer_conv(u, dw, db, g, b, ts):
    B, S, C = u.shape
    width = dw.shape[0]
    assert width - 1 <= 32
    sub = min(64, ts)
    hb = ts // 32
    return pl.pallas_call(
        functools.partial(_conv_kernel, width, ts, sub),
        grid=(B, S // ts),
        in_specs=[pl.BlockSpec((1, ts, C), lambda bi, i: (bi, i, 0)),
                  pl.BlockSpec((1, 32, C), lambda bi, i: (bi, jnp.maximum(i * hb - 1, 0), 0)),
                  _const_spec((width, C)), _const_spec((1, C)), _const_spec((1, C)), _const_spec((1, C))],
        out_specs=pl.BlockSpec((1, ts, C), lambda bi, i: (bi, i, 0)),
        out_shape=jax.ShapeDtypeStruct((B, S, C), BF16),
        scratch_shapes=[pltpu.VMEM((SUBLANES, ts + 32, C), F32)],
        compiler_params=_cparams("parallel", "arbitrary"),
        name="conv",
    )(u, u, dw, db.reshape(1, C), g.reshape(1, C), b.reshape(1, C))


def _hgrn_chunk(q, z, v, lb, st, skew_rhs):
    C, Dh = q.shape
    ls = jnp.minimum(z, 0.0) - jnp.log(1.0 + jnp.exp(-jnp.abs(z)))
    a_ = jnp.log(lb)
    b_ = jnp.log1p(-lb) + ls
    lf = jnp.maximum(a_, b_) + jnp.log(1.0 + jnp.exp(-jnp.abs(a_ - b_)))
    kk = (1.0 - lb) * _sigmoid(-z)

    row = lax.broadcasted_iota(I32, (C, C), 0)
    col = lax.broadcasted_iota(I32, (C, C), 1)
    tri = (col <= row).astype(BF16)
    lf_hi = lf.astype(BF16)
    lf_r = lf - lf_hi.astype(F32)
    lf_mid = lf_r.astype(BF16)
    lf_lo = (lf_r - lf_mid.astype(F32)).astype(BF16)
    A = (_dot(tri, lf_hi) + (_dot(tri, lf_mid) + _dot(tri, lf_lo))) * LOG2E

    o = _dot_nt((q * jnp.exp2(A)).astype(BF16), st.astype(BF16))

    rmod = lax.broadcasted_iota(I32, (C, Dh), 0) % RNN_SUB
    parts = []
    for d in range(RNN_SUB):
        if d == 0:
            e = q * kk
        else:
            a_s = pltpu.roll(A, d, axis=0)
            k_s = pltpu.roll(kk, d, axis=0)
            e = jnp.exp2(jnp.where(rmod >= d, A - a_s, NEG_BIG)) * q * k_s
        parts.append(e.astype(BF16))
    z = _dot(jnp.concatenate(parts, axis=1), skew_rhs)
    P = pltpu.roll(z, 0, 1, stride=1, stride_axis=0)[:, :C]

    blocks = [jnp.zeros((RNN_SUB, C), F32)]
    for i in range(1, C // RNN_SUB):
        r0 = i * RNN_SUB
        a_i = A[r0 - 1:r0, :]
        qi = q[r0:r0 + RNN_SUB] * jnp.exp2(A[r0:r0 + RNN_SUB] - a_i)
        ki = (kk[:r0] * jnp.exp2(a_i - A[:r0])).astype(BF16)
        ki = jnp.concatenate([ki, jnp.zeros((C - r0, Dh), BF16)], axis=0)
        blocks.append(_dot_nt(qi.astype(BF16), ki))
    P = P + jnp.concatenate(blocks, axis=0)
    o = o + _dot(P.astype(BF16), v.astype(BF16))

    a_last = A[C - 1:C, :]
    kd = kk * jnp.exp2(a_last - A)
    st_new = st * jnp.exp2(a_last) + _dot_tn(v.astype(BF16), kd.astype(BF16))
    return o, st_new


def _hgrn_kernel(n_chunks, q_ref, z_ref, v_ref, g_ref, lb_ref, ng_ref, skew_ref, o_ref, st_ref):
    @pl.when(pl.program_id(2) == 0)
    def _():
        st_ref[...] = jnp.zeros_like(st_ref)

    C = RNN_CHUNK

    def chunk_body(c, st):
        sl = pl.ds(pl.multiple_of(c * C, C), C)
        o, st = _hgrn_chunk(q_ref[0, sl, :].astype(F32), z_ref[0, sl, :], v_ref[0, sl, :].astype(F32),
                            lb_ref[...], st, skew_ref[...])
        o = o * lax.rsqrt(jnp.mean(o * o, axis=-1, keepdims=True) + LN_EPS) * ng_ref[...]
        o_ref[0, sl, :] = (o * g_ref[0, sl, :].astype(F32)).astype(BF16)
        return st

    st_ref[...] = lax.fori_loop(0, n_chunks, chunk_body, st_ref[...], unroll=min(RNN_UNROLL, n_chunks))


def _hgrn2(rq, rf, ri, rg, lb, norm_g, rows):
    B, S, DR = rq.shape
    H = N_RNN_HEADS
    Dh = DR // H
    blk = pl.BlockSpec((1, rows, Dh), lambda b, h, i: (b, i, h))
    vec = pl.BlockSpec((1, Dh), lambda b, h, i: (0, h))
    assert RNN_CHUNK + RNN_SUB <= LANES
    off = jnp.arange(RNN_SUB * Dh)[:, None] // Dh
    skew_rhs = (jnp.arange(LANES)[None, :] == (LANES - off) % LANES).astype(BF16)
    return pl.pallas_call(
        functools.partial(_hgrn_kernel, rows // RNN_CHUNK),
        grid=(B, H, S // rows),
        in_specs=[blk, blk, blk, blk, vec, vec, _const_spec(skew_rhs.shape)],
        out_specs=blk,
        out_shape=jax.ShapeDtypeStruct((B, S, DR), BF16),
        scratch_shapes=[pltpu.VMEM((Dh, Dh), F32)],
        compiler_params=_cparams("parallel", "parallel", "arbitrary"),
        name="hgrn2",
    )(rq, rf, ri, rg, lb.reshape(1, DR), norm_g.reshape(1, DR), skew_rhs)


def _dsa_kernel(topk, n_heads, aq_ref, c_ref, iq_ref, ik_ref, ikw_q_ref, bias_ref, wuv_ref, o_ref,
                cpad_ref, key_ref, dig_ref, mask_ref, m_ref, l_ref, acc_ref, thr_ref, need_ref):
    QB, KC = Q_BLOCK, KEY_CHUNK
    PAD = KC - 1
    qi = pl.program_id(1)
    n_chunks = qi // KC + 1
    n_total_chunks = key_ref.shape[0] // KC
    row = lax.broadcasted_iota(I32, (QB, QB), 0)
    col = lax.broadcasted_iota(I32, (QB, QB), 1)
    dl = c_ref.shape[2]

    @pl.when(qi == 0)
    def _():
        cpad_ref[0:PAD * QB, :] = jnp.zeros((PAD * QB, dl), BF16)
        cpad_ref[PAD * QB:, :] = c_ref[0]

    iq = iq_ref[0]
    q_st = jnp.concatenate([iq[:, h * IDX_HEAD_DIM:(h + 1) * IDX_HEAD_DIM] for h in range(N_IDX_HEADS)], axis=0)
    w_rows = ikw_q_ref[0].T[IDX_HEAD_DIM:IDX_HEAD_DIM + 8, :]
    kpos_minus_q = row - col

    def score_body(ci, carry):
        rows = pl.ds(pl.multiple_of(ci * (KC * QB), KC * QB), KC * QB)
        kj = ik_ref[0, rows, :][:, :IDX_HEAD_DIM].astype(BF16)
        s4 = _dot_nt(kj, q_st)
        for b in range(KC):
            j = ci * KC + b
            sc = jnp.zeros((QB, QB), F32)
            for h in range(N_IDX_HEADS):
                sc = sc + w_rows[h:h + 1, :] * jnp.maximum(s4[b * QB:(b + 1) * QB, h * QB:(h + 1) * QB], 0.0)
            sc = sc + 0.0
            bits = pltpu.bitcast(sc, I32)
            key = bits ^ ((bits >> 31) & 0x7FFFFFFF)
            key = jnp.where(kpos_minus_q <= (qi - j) * QB, key, INT_MIN)
            key_ref[j] = key
            for k in range(N_DIGITS):
                d = key >> (DIGIT_BITS * k)
                if k < N_DIGITS - 1:
                    d = (d & ((1 << DIGIT_BITS) - 1)) + DIGIT_MIN
                dig_ref[k, j] = d.astype(DIGIT)
        return carry

    lax.fori_loop(0, n_chunks, score_body, 0)

    def fill_body(ci, carry):
        for b in range(KC):
            for k in range(N_DIGITS):
                dig_ref[k, ci * KC + b] = jnp.full((QB, QB), DIGIT_MIN, DIGIT)
        return carry

    lax.fori_loop(n_chunks, n_total_chunks, fill_body, 0)

    def count_rows(acc):
        return jnp.sum(acc.astype(F32), axis=0, keepdims=True)

    def find_threshold(nc):
        one, zero = jnp.ones((), DIGIT), jnp.zeros((), DIGIT)

        def search_digit(k, need):
            def bit_body(it, cur):
                cand = cur + lax.shift_left(jnp.int32(1), DIGIT_BITS - 1 - it)
                cand_d = jnp.broadcast_to(cand, (QB, QB)).astype(DIGIT)
                acc = jnp.zeros((QB, QB), DIGIT)
                for j in range(nc * KC):
                    acc = acc + jnp.where(dig_ref[k, j] >= cand_d, one, zero)
                return jnp.where(count_rows(acc) >= need, cand, cur)

            return lax.fori_loop(0, DIGIT_BITS, bit_body, jnp.full((1, QB), DIGIT_MIN, I32))

        need = jnp.full((1, QB), float(topk), F32)
        thr = jnp.zeros((1, QB), I32)
        for k in reversed(range(N_DIGITS)):
            t = search_digit(k, need)
            t_d = jnp.broadcast_to(t, (QB, QB)).astype(DIGIT)
            acc = jnp.zeros((QB, QB), DIGIT)
            for j in range(nc * KC):
                x = dig_ref[k, j]
                acc = acc + jnp.where(x > t_d, one, zero)
                if k > 0:
                    dig_ref[k - 1, j] = jnp.where(x == t_d, dig_ref[k - 1, j], jnp.full((), DIGIT_MIN, DIGIT))
            need = need - count_rows(acc)
            thr = thr + (t if k == N_DIGITS - 1 else t - DIGIT_MIN) * (1 << (DIGIT_BITS * k))
        thr_ref[...] = thr
        need_ref[...] = need

    prev = 0
    for nc in sorted({min(n, n_total_chunks) for n in RADIX_VARIANTS} | {n_total_chunks}):
        pl.when((n_chunks > prev) & (n_chunks <= nc))(functools.partial(find_threshold, nc))
        prev = nc
    thr = jnp.maximum(thr_ref[...], INT_MIN + 1)
    thr_b = jnp.broadcast_to(thr, (QB, QB))
    need_b = jnp.broadcast_to(need_ref[...], (QB, QB))

    for p in range(PAD):
        mask_ref[p] = jnp.full((QB, QB), NEG_BIG, F32)
    tri = (col <= row).astype(BF16)

    def mask_body(ci, taken):
        keys = [key_ref[ci * KC + b] for b in range(KC)]
        eqs = [k == thr_b for k in keys]
        cnts = [_dot(tri, jnp.where(e, 1.0, 0.0).astype(BF16)) for e in eqs]
        for b in range(KC):
            tie = jnp.where(taken + cnts[b] <= need_b, 0.0, NEG_BIG)
            m = jnp.where(keys[b] > thr_b, 0.0, jnp.where(eqs[b], tie, NEG_BIG))
            mask_ref[PAD + ci * KC + b] = m.T
            taken = taken + cnts[b][QB - 1:QB, :]
        return taken

    lax.fori_loop(0, n_chunks, mask_body, jnp.zeros((1, QB), F32))

    q_all = jnp.concatenate([aq_ref[0, :, h * dl:(h + 1) * dl] for h in range(n_heads)], axis=0)
    n_far = qi // KC

    def att_step(start, near):
        cch = cpad_ref[pl.ds(pl.multiple_of(start * QB, QB), KC * QB), :]
        for g in range(0, n_heads, HEAD_GROUP):
            lg = _dot_nt(q_all[g * QB:(g + HEAD_GROUP) * QB], cch)
            rows, alphas = [], []
            for i in range(HEAD_GROUP):
                h = g + i
                blks = []
                for b in range(KC):
                    x = lg[i * QB:(i + 1) * QB, b * QB:(b + 1) * QB] + mask_ref[start + b]
                    if near and b == KC - 1:
                        x = x + bias_ref[0, h]
                    if near and b == KC - 2:
                        x = x + bias_ref[1, h]
                    blks.append(x)
                m_old = m_ref[h]
                mx = blks[0]
                for b in range(1, KC):
                    mx = jnp.maximum(mx, blks[b])
                m_new = jnp.maximum(m_old, jnp.broadcast_to(jnp.max(mx, axis=1, keepdims=True), (QB, QB)))
                alpha = jnp.exp2(m_old - m_new)
                ps = [jnp.exp2(blks[b] - m_new) for b in range(KC)]
                s = ps[0]
                for b in range(1, KC):
                    s = s + ps[b]
                l_ref[h] = alpha * l_ref[h] + s
                m_ref[h] = m_new
                rows.append(jnp.concatenate([p.astype(BF16) for p in ps], axis=1))
                alphas.append(alpha)
            pv = _dot(jnp.concatenate(rows, axis=0), cch)
            for i in range(HEAD_GROUP):
                sl = slice((g + i) * QB, (g + i + 1) * QB)
                acc_ref[sl, :] = alphas[i] * acc_ref[sl, :] + pv[i * QB:(i + 1) * QB, :]

    m_ref[...] = jnp.full(m_ref.shape, NEG_BIG, F32)
    l_ref[...] = jnp.zeros(l_ref.shape, F32)
    acc_ref[...] = jnp.zeros(acc_ref.shape, F32)

    def far_body(m, carry):
        att_step(qi - KC * (m + 1), False)
        return carry

    lax.fori_loop(0, n_far, far_body, 0)
    att_step(qi, True)

    out = jnp.zeros((QB, o_ref.shape[2]), F32)
    for h in range(n_heads):
        oh = acc_ref[h * QB:(h + 1) * QB, :] / jnp.sum(l_ref[h], axis=1, keepdims=True)
        out = out + _dot(oh.astype(BF16), wuv_ref[h])
    o_ref[0] = out.astype(BF16)


def _dsa_attention(aq, c, iq, ikw, bias_near, wuv_pad):
    B, S, DQ = aq.shape
    dl = c.shape[2]
    assert dl == Q_BLOCK
    assert (DQ // dl) % HEAD_GROUP == 0
    n_heads = DQ // dl
    d_att = wuv_pad.shape[2]
    topk = min(TOPK_MAX, S // 4)
    QB, KC = Q_BLOCK, KEY_CHUNK
    nkb = S // QB
    assert nkb % KC == 0
    return pl.pallas_call(
        functools.partial(_dsa_kernel, topk, n_heads),
        grid=(B, nkb),
        in_specs=[pl.BlockSpec((1, QB, DQ), lambda b, i: (b, i, 0)),
                  pl.BlockSpec((1, S, dl), lambda b, i: (b, 0, 0)),
                  pl.BlockSpec((1, QB, iq.shape[2]), lambda b, i: (b, i, 0)),
                  pl.BlockSpec((1, S, LANES), lambda b, i: (b, 0, 0)),
                  pl.BlockSpec((1, QB, LANES), lambda b, i: (b, i, 0)),
                  _const_spec(bias_near.shape), _const_spec(wuv_pad.shape)],
        out_specs=pl.BlockSpec((1, QB, d_att), lambda b, i: (b, i, 0)),
        out_shape=jax.ShapeDtypeStruct((B, S, d_att), BF16),
        scratch_shapes=[pltpu.VMEM(((nkb + KC - 1) * QB, dl), BF16),
                        pltpu.VMEM((nkb, QB, QB), I32), pltpu.VMEM((N_DIGITS, nkb, QB, QB), DIGIT),
                        pltpu.VMEM((nkb + KC - 1, QB, QB), F32),
                        pltpu.VMEM((n_heads, QB, QB), F32), pltpu.VMEM((n_heads, QB, QB), F32),
                        pltpu.VMEM((n_heads * QB, dl), F32), pltpu.VMEM((1, QB), I32),
                        pltpu.VMEM((1, QB), F32)],
        compiler_params=_cparams("parallel", "arbitrary"),
        name="dsa",
    )(aq, c, iq, ikw, ikw, bias_near, wuv_pad)


def _t5_bucket(n):
    nf = jnp.maximum(n, 1).astype(F32)
    large = MAX_EXACT + (jnp.log(nf / MAX_EXACT) / math.log(MAX_DISTANCE / MAX_EXACT)
                         * (N_BUCKETS - MAX_EXACT)).astype(I32)
    large = jnp.minimum(large, N_BUCKETS - 1)
    return jnp.where(n < MAX_EXACT, n, large)


def _near_bias(rel_bias):
    QB = Q_BLOCK
    assert QB >= MAX_DISTANCE
    r = jnp.arange(QB)[:, None] - jnp.arange(QB)[None, :]
    def lookup(bucket):
        one_hot = jax.nn.one_hot(bucket, N_BUCKETS, dtype=F32)
        return jnp.einsum('rcb,bh->rch', one_hot, rel_bias.astype(F32), precision=lax.Precision.HIGHEST)

    far = rel_bias[N_BUCKETS - 1]
    diag = lookup(_t5_bucket(jnp.maximum(r, 0))) - far
    prev = lookup(_t5_bucket(r + QB)) - far
    return (jnp.stack([diag, prev]).transpose(0, 3, 1, 2) * LOG2E).astype(F32)


def _merge_kernel(alpha, D, h_ref, u_ref, r_ref, a_ref, gt_ref, wc_ref, wr_ref, wa_ref, wo_ref, g_ref, b_ref,
                  o_ref):
    y = gt_ref[:, 0:D].astype(F32) * _dot(u_ref[...], wc_ref[...])
    y = y + gt_ref[:, D:2 * D].astype(F32) * _dot(r_ref[...], wr_ref[...])
    y = y + gt_ref[:, 2 * D:3 * D].astype(F32) * _dot(a_ref[...], wa_ref[...])
    z = alpha * h_ref[...] + _dot(y.astype(BF16), wo_ref[...])
    o_ref[...] = _ln_rows(z, g_ref[...], b_ref[...])


def _merge(h, u, r, a, gt, wc, wr, wa, wo, g, b, alpha, tm):
    T, D = h.shape
    row = lambda n: pl.BlockSpec((tm, n), lambda i: (i, 0))
    return pl.pallas_call(
        functools.partial(_merge_kernel, alpha, D),
        grid=(T // tm,),
        in_specs=[row(D), row(u.shape[1]), row(r.shape[1]), row(a.shape[1]), row(3 * D),
                  _const_spec(wc.shape), _const_spec(wr.shape), _const_spec(wa.shape), _const_spec(wo.shape),
                  _const_spec((1, D)), _const_spec((1, D))],
        out_specs=row(D),
        out_shape=jax.ShapeDtypeStruct((T, D), F32),
        compiler_params=_cparams("parallel"),
        name="merge",
    )(h, u, r, a, gt, wc, wr, wa, wo, g.reshape(1, D), b.reshape(1, D))


def _split_bf16(x):
    hi = x.astype(BF16)
    return hi, (x - hi.astype(F32)).astype(BF16)


def _moe_kernel(alpha, n_exp, de, h_ref, rw_hi_ref, rw_lo_ref, rb_ref, wgu_ref, wdn_ref, wsgu_ref, wsdn_ref,
                g_ref, b_ref, o_ref, xb_ref, gate_ref, acc_ref):
    e = pl.program_id(1)
    lane = lax.broadcasted_iota(I32, gate_ref.shape, 1)

    @pl.when(e == 0)
    def _():
        x = h_ref[...]
        x_hi, x_lo = _split_bf16(x)
        xb_ref[...] = x_hi
        logits = _dot(x_hi, rw_hi_ref[...]) + (_dot(x_lo, rw_hi_ref[...]) + _dot(x_hi, rw_lo_ref[...]))
        scores = _sigmoid(logits)
        sb = scores + rb_ref[...]
        sel = jnp.zeros(sb.shape, jnp.bool_)
        for _ in range(TOPK_EXPERTS):
            m = jnp.max(sb, axis=-1, keepdims=True)
            first = jnp.min(jnp.where(sb == m, lane, n_exp), axis=-1, keepdims=True)
            hit = lane == first
            sel = sel | hit
            sb = jnp.where(hit, -jnp.inf, sb)
        w = jnp.where(sel, scores, 0.0)
        gate_ref[...] = w / jnp.sum(w, axis=-1, keepdims=True) * ROUTE_SCALE
        ds = wsdn_ref.shape[0]
        s = _dot(x_hi, wsgu_ref[...])
        a_s, u_s = s[:, :ds], s[:, ds:]
        acc_ref[...] = _dot((a_s * _sigmoid(a_s) * u_s).astype(BF16), wsdn_ref[...])

    xb = xb_ref[...]
    n_grp = wgu_ref.shape[0]
    hmids = []
    for i in range(n_grp):
        au = _dot(xb, wgu_ref[i])
        a, u = au[:, :de], au[:, de:]
        ge = jnp.sum(jnp.where(lane == e * n_grp + i, gate_ref[...], 0.0), axis=-1, keepdims=True)
        hmids.append((a * _sigmoid(a) * u * ge).astype(BF16))
    acc_ref[...] += _dot(jnp.concatenate(hmids, axis=1), wdn_ref[...].reshape(n_grp * de, -1))

    @pl.when(e == n_exp // n_grp - 1)
    def _():
        o_ref[...] = _ln_rows(alpha * h_ref[...] + acc_ref[...], g_ref[...], b_ref[...])


def _moe(h, rw, rb, wgu, wdn, wsgu, wsdn, g, b, alpha, tm):
    T, D = h.shape
    n_exp, _, de2 = wgu.shape
    de = de2 // 2
    rw_hi, rw_lo = _split_bf16(rw)
    return pl.pallas_call(
        functools.partial(_moe_kernel, alpha, n_exp, de),
        grid=(T // tm, n_exp // MOE_EXPERT_GROUP),
        in_specs=[pl.BlockSpec((tm, D), lambda i, e: (i, 0)),
                  _const_spec(rw.shape), _const_spec(rw.shape), _const_spec((1, n_exp)),
                  pl.BlockSpec((MOE_EXPERT_GROUP, D, de2), lambda i, e: (e, 0, 0)),
                  pl.BlockSpec((MOE_EXPERT_GROUP, de, D), lambda i, e: (e, 0, 0)),
                  _const_spec(wsgu.shape), _const_spec(wsdn.shape),
                  _const_spec((1, D)), _const_spec((1, D))],
        out_specs=pl.BlockSpec((tm, D), lambda i, e: (i, 0)),
        out_shape=jax.ShapeDtypeStruct((T, D), F32),
        scratch_shapes=[pltpu.VMEM((tm, D), BF16), pltpu.VMEM((tm, n_exp), F32), pltpu.VMEM((tm, D), F32)],
        compiler_params=_cparams("parallel", "arbitrary"),
        name="moe",
    )(h, rw_hi, rw_lo, rb.reshape(1, n_exp), wgu, wdn, wsgu, wsdn, g.reshape(1, D), b.reshape(1, D))


def _pack_in_weights(w, b, dims):
    dc, dr, daq, dl, diq, dg = dims
    aligned = [2 * dc, dr, dr, dr, dr, daq, dl, diq]
    assert all(s % LANES == 0 for s in aligned) and IDX_HEAD_DIM + N_IDX_HEADS <= LANES
    split = sum(aligned) + IDX_HEAD_DIM + N_IDX_HEADS
    assert split + dg == w.shape[1]
    pad = LANES - IDX_HEAD_DIM - N_IDX_HEADS
    wb = w.astype(BF16)
    w_pack = jnp.concatenate([wb[:, :split], jnp.zeros((w.shape[0], pad), BF16), wb[:, split:]], axis=1)
    b_pack = jnp.concatenate([b[:split], jnp.zeros((pad,), b.dtype), b[split:]])
    return w_pack, b_pack.reshape(1, -1).astype(F32)


def kernel(x, ln_in_g, ln_in_b, w_in, b_in, conv_dw, conv_b, conv_ln_g, conv_ln_b, w_conv_proj, hgrn_gamma,
           hgrn_norm_g, w_rnn_proj, kv_norm_g, w_uv, w_att_proj, rel_bias, w_out, ln_mix_g, ln_mix_b,
           router_w, router_b, w_expert_gu, w_expert_dn, w_shared_gu, w_shared_dn, ln_ffn_g, ln_ffn_b):
    B, S, D = x.shape
    T = B * S
    depth = w_in.shape[0]
    dc = conv_dw.shape[2]
    dr = hgrn_gamma.shape[1]
    n_heads, dl, dv = w_uv.shape[1:]
    daq = n_heads * dl
    diq = N_IDX_HEADS * IDX_HEAD_DIM
    dims = (dc, dr, daq, dl, diq, 3 * D)
    alpha = (2 * depth) ** 0.25

    tm = min(512, T)
    lb_all = jnp.cumsum(jax.nn.softmax(hgrn_gamma.astype(F32), axis=0), axis=0)
    lb_all = lb_all - lb_all[0]
    bias_near = _near_bias(rel_bias)

    h = _layer_norm(x.reshape(T, D), ln_in_g, ln_in_b, tm)
    for l in range(depth):
        w_pack, b_pack = _pack_in_weights(w_in[l], b_in[l], dims)
        u, rq, rf, ri, rg, aq, c, iq, ikw, gt = _in_projection(h, w_pack, b_pack, kv_norm_g[l], dims, tm)

        def seq(t):
            return t.reshape(B, S, t.shape[-1])

        y_conv = _conformer_conv(seq(u), conv_dw[l], conv_b[l], conv_ln_g[l], conv_ln_b[l], min(512, S))
        y_rnn = _hgrn2(seq(rq), seq(rf), seq(ri), seq(rg), lb_all[l], hgrn_norm_g[l], min(1024, S))
        wuv_pad = jnp.zeros((n_heads, dl, n_heads * dv), F32)
        for hh in range(n_heads):
            wuv_pad = wuv_pad.at[hh, :, hh * dv:(hh + 1) * dv].set(w_uv[l, hh])
        y_att = _dsa_attention(seq(aq), seq(c), seq(iq), seq(ikw), bias_near, wuv_pad.astype(BF16))
        h = _merge(h, y_conv.reshape(T, -1), y_rnn.reshape(T, -1), y_att.reshape(T, -1), gt,
                   w_conv_proj[l].astype(BF16), w_rnn_proj[l].astype(BF16), w_att_proj[l].astype(BF16),
                   w_out[l].astype(BF16), ln_mix_g[l], ln_mix_b[l], alpha, tm)
        h = _moe(h, router_w[l], router_b[l], w_expert_gu[l].astype(BF16), w_expert_dn[l].astype(BF16),
                 w_shared_gu[l].astype(BF16), w_shared_dn[l].astype(BF16), ln_ffn_g[l], ln_ffn_b[l], alpha,
                 min(1024, T))
    return h.reshape(B, S, D)
```

```python
import functools
import math

import jax
import jax.numpy as jnp
from jax import lax
from jax.experimental import pallas as pl
from jax.experimental.pallas import tpu as pltpu

N_RNN_HEADS = 4
N_IDX_HEADS = 4
IDX_HEAD_DIM = 64
TOPK_MAX = 256
N_BUCKETS = 32
MAX_EXACT = N_BUCKETS // 2
MAX_DISTANCE = 128
TOPK_EXPERTS = 4
ROUTE_SCALE = 2.5
LN_EPS = 1e-5

LANES = 128
SUBLANES = 8
VMEM_LIMIT_BYTES = 56 * 1024 * 1024

Q_BLOCK = 128
KEY_CHUNK = 4
RADIX_VARIANTS = (2, 4, 6)
HEAD_GROUP = 8
MOE_EXPERT_GROUP = 4
RNN_CHUNK = 64
RNN_SUB = 16
RNN_UNROLL = 16
NEG_BIG = -1e30
INT_MIN = -2147483648
DIGIT_BITS = 16
N_DIGITS = 32 // DIGIT_BITS
DIGIT = jnp.int16
DIGIT_MIN = -(1 << (DIGIT_BITS - 1))
assert N_DIGITS == 2
LOG2E = math.log2(math.e)

BF16 = jnp.bfloat16
F32 = jnp.float32
I16 = jnp.int16
I32 = jnp.int32


def _cparams(*sem):
    return pltpu.CompilerParams(dimension_semantics=sem, vmem_limit_bytes=VMEM_LIMIT_BYTES)


def _const_spec(shape):
    nd = len(shape)
    return pl.BlockSpec(shape, lambda *_: (0,) * nd, pipeline_mode=pl.Buffered(1))


def _ln_rows(x, g, b):
    mu = jnp.mean(x, axis=-1, keepdims=True)
    xc = x - mu
    var = jnp.mean(xc * xc, axis=-1, keepdims=True)
    return xc * lax.rsqrt(var + LN_EPS) * g + b


def _sigmoid(x):
    return 1.0 / (1.0 + jnp.exp(-x))


def _dot(a, b):
    return jnp.dot(a, b, preferred_element_type=F32)


def _dot_nt(a, b):
    return lax.dot_general(a, b, (((1,), (1,)), ((), ())), preferred_element_type=F32)


def _dot_tn(a, b):
    return lax.dot_general(a, b, (((0,), (0,)), ((), ())), preferred_element_type=F32)


def _ln_kernel(x_ref, g_ref, b_ref, o_ref):
    o_ref[...] = _ln_rows(x_ref[...], g_ref[...], b_ref[...])


def _layer_norm(x, g, b, tm):
    T, D = x.shape
    return pl.pallas_call(
        _ln_kernel,
        grid=(T // tm,),
        in_specs=[pl.BlockSpec((tm, D), lambda i: (i, 0)), _const_spec((1, D)), _const_spec((1, D))],
        out_specs=pl.BlockSpec((tm, D), lambda i: (i, 0)),
        out_shape=jax.ShapeDtypeStruct((T, D), F32),
        compiler_params=_cparams("parallel"),
        name="ln_in",
    )(x, g.reshape(1, D), b.reshape(1, D))


def _proj_kernel(dims, h_ref, w_ref, b_ref, kvg_ref,
                 u_ref, rq_ref, rf_ref, ri_ref, rg_ref, aq_ref, c_ref, iq_ref, ikw_ref, gt_ref):
    dc, dr, daq, dl, diq, dg = dims
    x = h_ref[...].astype(BF16)
    off = [0]

    def seg(n):
        o = off[0]
        off[0] = o + n
        return _dot(x, w_ref[:, o:o + n]) + b_ref[:, o:o + n]

    a = seg(dc)
    gate = seg(dc)
    u_ref[...] = (a * _sigmoid(gate)).astype(BF16)
    rq_ref[...] = seg(dr).astype(BF16)
    rf_ref[...] = seg(dr)
    ri_ref[...] = seg(dr).astype(BF16)
    rg_ref[...] = _sigmoid(seg(dr)).astype(BF16)
    scale = dl ** -0.5 * LOG2E
    for j in range(daq // 512):
        aq_ref[:, j * 512:(j + 1) * 512] = (seg(512) * scale).astype(BF16)
    c = seg(dl)
    c = c * lax.rsqrt(jnp.mean(c * c, axis=-1, keepdims=True) + LN_EPS) * kvg_ref[...]
    c_ref[...] = c.astype(BF16)
    iq_ref[...] = seg(diq).astype(BF16)
    ikw_ref[...] = seg(LANES)
    for j in range(dg // 512):
        gt_ref[:, j * 512:(j + 1) * 512] = _sigmoid(seg(512)).astype(BF16)


def _in_projection(h, w, b, kv_g, dims, tm):
    T, D = h.shape
    dc, dr, daq, dl, diq, dg = dims
    n_pad = w.shape[1]
    widths = [(dc, BF16), (dr, BF16), (dr, F32), (dr, BF16), (dr, BF16), (daq, BF16), (dl, BF16),
              (diq, BF16), (LANES, F32), (dg, BF16)]
    return pl.pallas_call(
        functools.partial(_proj_kernel, dims),
        grid=(T // tm,),
        in_specs=[pl.BlockSpec((tm, D), lambda i: (i, 0)), _const_spec((D, n_pad)),
                  _const_spec((1, n_pad)), _const_spec((1, dl))],
        out_specs=[pl.BlockSpec((tm, n), lambda i: (i, 0)) for n, _ in widths],
        out_shape=[jax.ShapeDtypeStruct((T, n), dt) for n, dt in widths],
        compiler_params=_cparams("parallel"),
        name="in_proj",
    )(h, w, b, kv_g.reshape(1, dl))


def _conv_kernel(width, ts, sub, u_ref, halo_ref, dw_ref, db_ref, g_ref, b_ref, o_ref, buf_ref):
    i = pl.program_id(1)
    halo = halo_ref[0].astype(F32)
    buf_ref[0, 0:32, :] = jnp.where(i > 0, halo, 0.0)
    buf_ref[0, 32:32 + ts, :] = u_ref[0].astype(F32)
    for r in range(1, SUBLANES):
        buf_ref[r, 0:ts + 32 - SUBLANES, :] = buf_ref[0, r:r + ts + 32 - SUBLANES, :]
    base = 32 - (width - 1)
    for s in range(ts // sub):
        acc = jnp.zeros((sub, u_ref.shape[2]), F32) + db_ref[...]
        for j in range(width):
            r = (base + j) % SUBLANES
            r0 = base + j - r + s * sub
            acc = acc + buf_ref[r, r0:r0 + sub, :] * dw_ref[j:j + 1, :]
        y = _ln_rows(acc, g_ref[...], b_ref[...])
        o_ref[0, s * sub:(s + 1) * sub, :] = (y * _sigmoid(y)).astype(BF16)


def _conformer_conv(u, dw, db, g, b, ts):
    B, S, C = u.shape
    width = dw.shape[0]
    assert width - 1 <= 32
    sub = min(64, ts)
    hb = ts // 32
    return pl.pallas_call(
        functools.partial(_conv_kernel, width, ts, sub),
        grid=(B, S // ts),
        in_specs=[pl.BlockSpec((1, ts, C), lambda bi, i: (bi, i, 0)),
                  pl.BlockSpec((1, 32, C), lambda bi, i: (bi, jnp.maximum(i * hb - 1, 0), 0)),
                  _const_spec((width, C)), _const_spec((1, C)), _const_spec((1, C)), _const_spec((1, C))],
        out_specs=pl.BlockSpec((1, ts, C), lambda bi, i: (bi, i, 0)),
        out_shape=jax.ShapeDtypeStruct((B, S, C), BF16),
        scratch_shapes=[pltpu.VMEM((SUBLANES, ts + 32, C), F32)],
        compiler_params=_cparams("parallel", "arbitrary"),
        name="conv",
    )(u, u, dw, db.reshape(1, C), g.reshape(1, C), b.reshape(1, C))


def _hgrn_chunk(q, z, v, lb, st, skew_rhs):
    C, Dh = q.shape
    ls = jnp.minimum(z, 0.0) - jnp.log(1.0 + jnp.exp(-jnp.abs(z)))
    a_ = jnp.log(lb)
    b_ = jnp.log1p(-lb) + ls
    lf = jnp.maximum(a_, b_) + jnp.log(1.0 + jnp.exp(-jnp.abs(a_ - b_)))
    kk = (1.0 - lb) * _sigmoid(-z)

    row = lax.broadcasted_iota(I32, (C, C), 0)
    col = lax.broadcasted_iota(I32, (C, C), 1)
    tri = (col <= row).astype(BF16)
    lf_hi = lf.astype(BF16)
    lf_r = lf - lf_hi.astype(F32)
    lf_mid = lf_r.astype(BF16)
    lf_lo = (lf_r - lf_mid.astype(F32)).astype(BF16)
    A = (_dot(tri, lf_hi) + (_dot(tri, lf_mid) + _dot(tri, lf_lo))) * LOG2E

    o = _dot_nt((q * jnp.exp2(A)).astype(BF16), st.astype(BF16))

    rmod = lax.broadcasted_iota(I32, (C, Dh), 0) % RNN_SUB
    G = A - jnp.log2(kk)
    parts = []
    for d in range(RNN_SUB):
        if d == 0:
            e = q * kk
        else:
            e = jnp.exp2(jnp.where(rmod >= d, A - pltpu.roll(G, d, axis=0), NEG_BIG)) * q
        parts.append(e.astype(BF16))
    z = _dot(jnp.concatenate(parts, axis=1), skew_rhs)
    P = pltpu.roll(z, 0, 1, stride=1, stride_axis=0)[:, :C]

    blocks = [jnp.zeros((RNN_SUB, C), F32)]
    for i in range(1, C // RNN_SUB):
        r0 = i * RNN_SUB
        a_i = A[r0 - 1:r0, :]
        qi = q[r0:r0 + RNN_SUB] * jnp.exp2(A[r0:r0 + RNN_SUB] - a_i)
        ki = (kk[:r0] * jnp.exp2(a_i - A[:r0])).astype(BF16)
        ki = jnp.concatenate([ki, jnp.zeros((C - r0, Dh), BF16)], axis=0)
        blocks.append(_dot_nt(qi.astype(BF16), ki))
    P = P + jnp.concatenate(blocks, axis=0)
    o = o + _dot(P.astype(BF16), v.astype(BF16))

    a_last = A[C - 1:C, :]
    kd = kk * jnp.exp2(a_last - A)
    st_new = st * jnp.exp2(a_last) + _dot_tn(v.astype(BF16), kd.astype(BF16))
    return o, st_new


def _hgrn_kernel(n_chunks, q_ref, z_ref, v_ref, g_ref, lb_ref, ng_ref, skew_ref, o_ref, st_ref):
    @pl.when(pl.program_id(2) == 0)
    def _():
        st_ref[...] = jnp.zeros_like(st_ref)

    C = RNN_CHUNK

    def chunk_body(c, st):
        sl = pl.ds(pl.multiple_of(c * C, C), C)
        o, st = _hgrn_chunk(q_ref[0, sl, :].astype(F32), z_ref[0, sl, :], v_ref[0, sl, :].astype(F32),
                            lb_ref[...], st, skew_ref[...])
        o = o * lax.rsqrt(jnp.mean(o * o, axis=-1, keepdims=True) + LN_EPS) * ng_ref[...]
        o_ref[0, sl, :] = (o * g_ref[0, sl, :].astype(F32)).astype(BF16)
        return st

    st_ref[...] = lax.fori_loop(0, n_chunks, chunk_body, st_ref[...], unroll=min(RNN_UNROLL, n_chunks))


def _hgrn2(rq, rf, ri, rg, lb, norm_g, rows):
    B, S, DR = rq.shape
    H = N_RNN_HEADS
    Dh = DR // H
    blk = pl.BlockSpec((1, rows, Dh), lambda b, h, i: (b, i, h))
    vec = pl.BlockSpec((1, Dh), lambda b, h, i: (0, h))
    assert RNN_CHUNK + RNN_SUB <= LANES
    off = jnp.arange(RNN_SUB * Dh)[:, None] // Dh
    skew_rhs = (jnp.arange(LANES)[None, :] == (LANES - off) % LANES).astype(BF16)
    return pl.pallas_call(
        functools.partial(_hgrn_kernel, rows // RNN_CHUNK),
        grid=(B, H, S // rows),
        in_specs=[blk, blk, blk, blk, vec, vec, _const_spec(skew_rhs.shape)],
        out_specs=blk,
        out_shape=jax.ShapeDtypeStruct((B, S, DR), BF16),
        scratch_shapes=[pltpu.VMEM((Dh, Dh), F32)],
        compiler_params=_cparams("parallel", "parallel", "arbitrary"),
        name="hgrn2",
    )(rq, rf, ri, rg, lb.reshape(1, DR), norm_g.reshape(1, DR), skew_rhs)


def _dsa_kernel(topk, n_heads, aq_ref, c_ref, iq_ref, ik_ref, ikw_q_ref, bias_ref, wuv_ref, o_ref,
                cpad_ref, key_ref, dig_ref, mask_ref, m_ref, l_ref, acc_ref, thr_ref, need_ref):
    QB, KC = Q_BLOCK, KEY_CHUNK
    PAD = KC - 1
    qi = pl.program_id(1)
    n_chunks = qi // KC + 1
    n_total_chunks = key_ref.shape[0] // KC
    row = lax.broadcasted_iota(I32, (QB, QB), 0)
    col = lax.broadcasted_iota(I32, (QB, QB), 1)
    dl = c_ref.shape[2]

    @pl.when(qi == 0)
    def _():
        cpad_ref[0:PAD * QB, :] = jnp.zeros((PAD * QB, dl), BF16)
        cpad_ref[PAD * QB:, :] = c_ref[0]

    iq = iq_ref[0]
    q_st = jnp.concatenate([iq[:, h * IDX_HEAD_DIM:(h + 1) * IDX_HEAD_DIM] for h in range(N_IDX_HEADS)], axis=0)
    w_rows = ikw_q_ref[0].T[IDX_HEAD_DIM:IDX_HEAD_DIM + 8, :]
    kpos_minus_q = row - col

    TRIP = 2 if n_total_chunks % 2 == 0 else 1
    n_trips = (n_chunks + TRIP - 1) // TRIP

    def score_chunk(ci):
        rows = pl.ds(pl.multiple_of(ci * (KC * QB), KC * QB), KC * QB)
        kj = ik_ref[0, rows, :][:, :IDX_HEAD_DIM].astype(BF16)
        s4 = _dot_nt(kj, q_st)
        for b in range(KC):
            j = ci * KC + b
            sc = jnp.zeros((QB, QB), F32)
            for h in range(N_IDX_HEADS):
                sc = sc + w_rows[h:h + 1, :] * jnp.maximum(s4[b * QB:(b + 1) * QB, h * QB:(h + 1) * QB], 0.0)
            sc = sc + 0.0
            bits = pltpu.bitcast(sc, I32)
            key = bits ^ ((bits >> 31) & 0x7FFFFFFF)
            key = jnp.where(kpos_minus_q <= (qi - j) * QB, key, INT_MIN)
            key_ref[j] = key
            for k in range(N_DIGITS):
                d = key >> (DIGIT_BITS * k)
                if k < N_DIGITS - 1:
                    d = (d & ((1 << DIGIT_BITS) - 1)) + DIGIT_MIN
                dig_ref[k, j] = d.astype(DIGIT)

    def score_body(t, carry):
        for u in range(TRIP):
            score_chunk(t * TRIP + u)
        return carry

    lax.fori_loop(0, n_trips, score_body, 0)

    def fill_body(ci, carry):
        for b in range(KC):
            for k in range(N_DIGITS):
                dig_ref[k, ci * KC + b] = jnp.full((QB, QB), DIGIT_MIN, DIGIT)
        return carry

    lax.fori_loop(n_trips * TRIP, n_total_chunks, fill_body, 0)

    def count_rows(acc):
        return jnp.sum(acc.astype(F32), axis=0, keepdims=True)

    def find_threshold(nc):
        one, zero = jnp.ones((), DIGIT), jnp.zeros((), DIGIT)

        def search_digit(k, need):
            def bit_body(it, cur):
                cand = cur + lax.shift_left(jnp.int32(1), DIGIT_BITS - 1 - it)
                cand_d = jnp.broadcast_to(cand, (QB, QB)).astype(DIGIT)
                acc = jnp.zeros((QB, QB), DIGIT)
                for j in range(nc * KC):
                    acc = acc + jnp.where(dig_ref[k, j] >= cand_d, one, zero)
                return jnp.where(count_rows(acc) >= need, cand, cur)

            return lax.fori_loop(0, DIGIT_BITS, bit_body, jnp.full((1, QB), DIGIT_MIN, I32))

        need = jnp.full((1, QB), float(topk), F32)
        thr = jnp.zeros((1, QB), I32)
        for k in reversed(range(N_DIGITS)):
            t = search_digit(k, need)
            t_d = jnp.broadcast_to(t, (QB, QB)).astype(DIGIT)
            acc = jnp.zeros((QB, QB), DIGIT)
            for j in range(nc * KC):
                x = dig_ref[k, j]
                acc = acc + jnp.where(x > t_d, one, zero)
                if k > 0:
                    dig_ref[k - 1, j] = jnp.where(x == t_d, dig_ref[k - 1, j], jnp.full((), DIGIT_MIN, DIGIT))
            need = need - count_rows(acc)
            thr = thr + (t if k == N_DIGITS - 1 else t - DIGIT_MIN) * (1 << (DIGIT_BITS * k))
        thr_ref[...] = thr
        need_ref[...] = need

    prev = 0
    for nc in sorted({min(n, n_total_chunks) for n in RADIX_VARIANTS} | {n_total_chunks}):
        pl.when((n_chunks > prev) & (n_chunks <= nc))(functools.partial(find_threshold, nc))
        prev = nc
    thr = jnp.maximum(thr_ref[...], INT_MIN + 1)
    thr_b = jnp.broadcast_to(thr, (QB, QB))
    need_b = jnp.broadcast_to(need_ref[...], (QB, QB))

    for p in range(PAD):
        mask_ref[p] = jnp.full((QB, QB), NEG_BIG, F32)
    tri = (col <= row).astype(BF16)

    def mask_body(t, taken):
        j0 = t * (TRIP * KC)
        keys = [key_ref[j0 + b] for b in range(TRIP * KC)]
        eqs = [k == thr_b for k in keys]
        cnts = [_dot(tri, jnp.where(e, 1.0, 0.0).astype(BF16)) for e in eqs]
        for b in range(TRIP * KC):
            tie = jnp.where(taken + cnts[b] <= need_b, 0.0, NEG_BIG)
            m = jnp.where(keys[b] > thr_b, 0.0, jnp.where(eqs[b], tie, NEG_BIG))
            mask_ref[PAD + j0 + b] = m.T
            taken = taken + cnts[b][QB - 1:QB, :]
        return taken

    lax.fori_loop(0, n_trips, mask_body, jnp.zeros((1, QB), F32))

    q_all = jnp.concatenate([aq_ref[0, :, h * dl:(h + 1) * dl] for h in range(n_heads)], axis=0)
    n_far = qi // KC

    def att_step(start, near):
        cch = cpad_ref[pl.ds(pl.multiple_of(start * QB, QB), KC * QB), :]
        for g in range(0, n_heads, HEAD_GROUP):
            lg = _dot_nt(q_all[g * QB:(g + HEAD_GROUP) * QB], cch)
            rows, alphas = [], []
            for i in range(HEAD_GROUP):
                h = g + i
                blks = []
                for b in range(KC):
                    x = lg[i * QB:(i + 1) * QB, b * QB:(b + 1) * QB] + mask_ref[start + b]
                    if near and b == KC - 1:
                        x = x + bias_ref[0, h]
                    if near and b == KC - 2:
                        x = x + bias_ref[1, h]
                    blks.append(x)
                m_old = m_ref[h]
                mx = blks[0]
                for b in range(1, KC):
                    mx = jnp.maximum(mx, blks[b])
                m_new = jnp.maximum(m_old, jnp.broadcast_to(jnp.max(mx, axis=1, keepdims=True), (QB, QB)))
                alpha = jnp.exp2(m_old - m_new)
                ps = [jnp.exp2(blks[b] - m_new) for b in range(KC)]
                s = ps[0]
                for b in range(1, KC):
                    s = s + ps[b]
                l_ref[h] = alpha * l_ref[h] + s
                m_ref[h] = m_new
                rows.append(jnp.concatenate([p.astype(BF16) for p in ps], axis=1))
                alphas.append(alpha)
            pv = _dot(jnp.concatenate(rows, axis=0), cch)
            for i in range(HEAD_GROUP):
                sl = slice((g + i) * QB, (g + i + 1) * QB)
                acc_ref[sl, :] = alphas[i] * acc_ref[sl, :] + pv[i * QB:(i + 1) * QB, :]

    m_ref[...] = jnp.full(m_ref.shape, NEG_BIG, F32)
    l_ref[...] = jnp.zeros(l_ref.shape, F32)
    acc_ref[...] = jnp.zeros(acc_ref.shape, F32)

    def far_body(m, carry):
        att_step(qi - KC * (m + 1), False)
        return carry

    lax.fori_loop(0, n_far, far_body, 0)
    att_step(qi, True)

    out = jnp.zeros((QB, o_ref.shape[2]), F32)
    for h in range(n_heads):
        oh = acc_ref[h * QB:(h + 1) * QB, :] / jnp.sum(l_ref[h], axis=1, keepdims=True)
        out = out + _dot(oh.astype(BF16), wuv_ref[h])
    o_ref[0] = out.astype(BF16)


def _dsa_attention(aq, c, iq, ikw, bias_near, wuv_pad):
    B, S, DQ = aq.shape
    dl = c.shape[2]
    assert dl == Q_BLOCK
    assert (DQ // dl) % HEAD_GROUP == 0
    n_heads = DQ // dl
    d_att = wuv_pad.shape[2]
    topk = min(TOPK_MAX, S // 4)
    QB, KC = Q_BLOCK, KEY_CHUNK
    nkb = S // QB
    assert nkb % KC == 0
    return pl.pallas_call(
        functools.partial(_dsa_kernel, topk, n_heads),
        grid=(B, nkb),
        in_specs=[pl.BlockSpec((1, QB, DQ), lambda b, i: (b, i, 0)),
                  pl.BlockSpec((1, S, dl), lambda b, i: (b, 0, 0)),
                  pl.BlockSpec((1, QB, iq.shape[2]), lambda b, i: (b, i, 0)),
                  pl.BlockSpec((1, S, LANES), lambda b, i: (b, 0, 0)),
                  pl.BlockSpec((1, QB, LANES), lambda b, i: (b, i, 0)),
                  _const_spec(bias_near.shape), _const_spec(wuv_pad.shape)],
        out_specs=pl.BlockSpec((1, QB, d_att), lambda b, i: (b, i, 0)),
        out_shape=jax.ShapeDtypeStruct((B, S, d_att), BF16),
        scratch_shapes=[pltpu.VMEM(((nkb + KC - 1) * QB, dl), BF16),
                        pltpu.VMEM((nkb, QB, QB), I32), pltpu.VMEM((N_DIGITS, nkb, QB, QB), DIGIT),
                        pltpu.VMEM((nkb + KC - 1, QB, QB), F32),
                        pltpu.VMEM((n_heads, QB, QB), F32), pltpu.VMEM((n_heads, QB, QB), F32),
                        pltpu.VMEM((n_heads * QB, dl), F32), pltpu.VMEM((1, QB), I32),
                        pltpu.VMEM((1, QB), F32)],
        compiler_params=_cparams("parallel", "arbitrary"),
        name="dsa",
    )(aq, c, iq, ikw, ikw, bias_near, wuv_pad)


def _t5_bucket(n):
    nf = jnp.maximum(n, 1).astype(F32)
    large = MAX_EXACT + (jnp.log(nf / MAX_EXACT) / math.log(MAX_DISTANCE / MAX_EXACT)
                         * (N_BUCKETS - MAX_EXACT)).astype(I32)
    large = jnp.minimum(large, N_BUCKETS - 1)
    return jnp.where(n < MAX_EXACT, n, large)


def _near_bias(rel_bias):
    QB = Q_BLOCK
    assert QB >= MAX_DISTANCE
    r = jnp.arange(QB)[:, None] - jnp.arange(QB)[None, :]
    def lookup(bucket):
        one_hot = jax.nn.one_hot(bucket, N_BUCKETS, dtype=F32)
        return jnp.einsum('rcb,bh->rch', one_hot, rel_bias.astype(F32), precision=lax.Precision.HIGHEST)

    far = rel_bias[N_BUCKETS - 1]
    diag = lookup(_t5_bucket(jnp.maximum(r, 0))) - far
    prev = lookup(_t5_bucket(r + QB)) - far
    return (jnp.stack([diag, prev]).transpose(0, 3, 1, 2) * LOG2E).astype(F32)


def _merge_kernel(alpha, D, h_ref, u_ref, r_ref, a_ref, gt_ref, wc_ref, wr_ref, wa_ref, wo_ref, g_ref, b_ref,
                  o_ref):
    y = gt_ref[:, 0:D].astype(F32) * _dot(u_ref[...], wc_ref[...])
    y = y + gt_ref[:, D:2 * D].astype(F32) * _dot(r_ref[...], wr_ref[...])
    y = y + gt_ref[:, 2 * D:3 * D].astype(F32) * _dot(a_ref[...], wa_ref[...])
    z = alpha * h_ref[...] + _dot(y.astype(BF16), wo_ref[...])
    o_ref[...] = _ln_rows(z, g_ref[...], b_ref[...])


def _merge(h, u, r, a, gt, wc, wr, wa, wo, g, b, alpha, tm):
    T, D = h.shape
    row = lambda n: pl.BlockSpec((tm, n), lambda i: (i, 0))
    return pl.pallas_call(
        functools.partial(_merge_kernel, alpha, D),
        grid=(T // tm,),
        in_specs=[row(D), row(u.shape[1]), row(r.shape[1]), row(a.shape[1]), row(3 * D),
                  _const_spec(wc.shape), _const_spec(wr.shape), _const_spec(wa.shape), _const_spec(wo.shape),
                  _const_spec((1, D)), _const_spec((1, D))],
        out_specs=row(D),
        out_shape=jax.ShapeDtypeStruct((T, D), F32),
        compiler_params=_cparams("parallel"),
        name="merge",
    )(h, u, r, a, gt, wc, wr, wa, wo, g.reshape(1, D), b.reshape(1, D))


def _split_bf16(x):
    hi = x.astype(BF16)
    return hi, (x - hi.astype(F32)).astype(BF16)


def _moe_kernel(alpha, n_exp, de, h_ref, rw_hi_ref, rw_lo_ref, rb_ref, wgu_ref, wdn_ref, wsgu_ref, wsdn_ref,
                g_ref, b_ref, o_ref, xb_ref, gate_ref, acc_ref):
    e = pl.program_id(1)
    lane = lax.broadcasted_iota(I32, gate_ref.shape, 1)

    @pl.when(e == 0)
    def _():
        x = h_ref[...]
        x_hi, x_lo = _split_bf16(x)
        xb_ref[...] = x_hi
        logits = _dot(x_hi, rw_hi_ref[...]) + (_dot(x_lo, rw_hi_ref[...]) + _dot(x_hi, rw_lo_ref[...]))
        scores = _sigmoid(logits)
        sb = scores + rb_ref[...]
        sel = jnp.zeros(sb.shape, jnp.bool_)
        for _ in range(TOPK_EXPERTS):
            m = jnp.max(sb, axis=-1, keepdims=True)
            first = jnp.min(jnp.where(sb == m, lane, n_exp), axis=-1, keepdims=True)
            hit = lane == first
            sel = sel | hit
            sb = jnp.where(hit, -jnp.inf, sb)
        w = jnp.where(sel, scores, 0.0)
        gate_ref[...] = w / jnp.sum(w, axis=-1, keepdims=True) * ROUTE_SCALE
        ds = wsdn_ref.shape[0]
        s = _dot(x_hi, wsgu_ref[...])
        a_s, u_s = s[:, :ds], s[:, ds:]
        acc_ref[...] = _dot((a_s * _sigmoid(a_s) * u_s).astype(BF16), wsdn_ref[...])

    xb = xb_ref[...]
    n_grp = wgu_ref.shape[0]
    hmids = []
    for i in range(n_grp):
        au = _dot(xb, wgu_ref[i])
        a, u = au[:, :de], au[:, de:]
        ge = jnp.sum(jnp.where(lane == e * n_grp + i, gate_ref[...], 0.0), axis=-1, keepdims=True)
        hmids.append((a * _sigmoid(a) * u * ge).astype(BF16))
    acc_ref[...] += _dot(jnp.concatenate(hmids, axis=1), wdn_ref[...].reshape(n_grp * de, -1))

    @pl.when(e == n_exp // n_grp - 1)
    def _():
        o_ref[...] = _ln_rows(alpha * h_ref[...] + acc_ref[...], g_ref[...], b_ref[...])


def _moe(h, rw, rb, wgu, wdn, wsgu, wsdn, g, b, alpha, tm):
    T, D = h.shape
    n_exp, _, de2 = wgu.shape
    de = de2 // 2
    rw_hi, rw_lo = _split_bf16(rw)
    return pl.pallas_call(
        functools.partial(_moe_kernel, alpha, n_exp, de),
        grid=(T // tm, n_exp // MOE_EXPERT_GROUP),
        in_specs=[pl.BlockSpec((tm, D), lambda i, e: (i, 0)),
                  _const_spec(rw.shape), _const_spec(rw.shape), _const_spec((1, n_exp)),
                  pl.BlockSpec((MOE_EXPERT_GROUP, D, de2), lambda i, e: (e, 0, 0)),
                  pl.BlockSpec((MOE_EXPERT_GROUP, de, D), lambda i, e: (e, 0, 0)),
                  _const_spec(wsgu.shape), _const_spec(wsdn.shape),
                  _const_spec((1, D)), _const_spec((1, D))],
        out_specs=pl.BlockSpec((tm, D), lambda i, e: (i, 0)),
        out_shape=jax.ShapeDtypeStruct((T, D), F32),
        scratch_shapes=[pltpu.VMEM((tm, D), BF16), pltpu.VMEM((tm, n_exp), F32), pltpu.VMEM((tm, D), F32)],
        compiler_params=_cparams("parallel", "arbitrary"),
        name="moe",
    )(h, rw_hi, rw_lo, rb.reshape(1, n_exp), wgu, wdn, wsgu, wsdn, g.reshape(1, D), b.reshape(1, D))


def _pack_in_weights(w, b, dims):
    dc, dr, daq, dl, diq, dg = dims
    aligned = [2 * dc, dr, dr, dr, dr, daq, dl, diq]
    assert all(s % LANES == 0 for s in aligned) and IDX_HEAD_DIM + N_IDX_HEADS <= LANES
    split = sum(aligned) + IDX_HEAD_DIM + N_IDX_HEADS
    assert split + dg == w.shape[1]
    pad = LANES - IDX_HEAD_DIM - N_IDX_HEADS
    wb = w.astype(BF16)
    w_pack = jnp.concatenate([wb[:, :split], jnp.zeros((w.shape[0], pad), BF16), wb[:, split:]], axis=1)
    b_pack = jnp.concatenate([b[:split], jnp.zeros((pad,), b.dtype), b[split:]])
    return w_pack, b_pack.reshape(1, -1).astype(F32)


def kernel(x, ln_in_g, ln_in_b, w_in, b_in, conv_dw, conv_b, conv_ln_g, conv_ln_b, w_conv_proj, hgrn_gamma,
           hgrn_norm_g, w_rnn_proj, kv_norm_g, w_uv, w_att_proj, rel_bias, w_out, ln_mix_g, ln_mix_b,
           router_w, router_b, w_expert_gu, w_expert_dn, w_shared_gu, w_shared_dn, ln_ffn_g, ln_ffn_b):
    B, S, D = x.shape
    T = B * S
    depth = w_in.shape[0]
    dc = conv_dw.shape[2]
    dr = hgrn_gamma.shape[1]
    n_heads, dl, dv = w_uv.shape[1:]
    daq = n_heads * dl
    diq = N_IDX_HEADS * IDX_HEAD_DIM
    dims = (dc, dr, daq, dl, diq, 3 * D)
    alpha = (2 * depth) ** 0.25

    tm = min(512, T)
    lb_all = jnp.cumsum(jax.nn.softmax(hgrn_gamma.astype(F32), axis=0), axis=0)
    lb_all = lb_all - lb_all[0]
    bias_near = _near_bias(rel_bias)

    h = _layer_norm(x.reshape(T, D), ln_in_g, ln_in_b, tm)
    for l in range(depth):
        w_pack, b_pack = _pack_in_weights(w_in[l], b_in[l], dims)
        u, rq, rf, ri, rg, aq, c, iq, ikw, gt = _in_projection(h, w_pack, b_pack, kv_norm_g[l], dims, tm)

        def seq(t):
            return t.reshape(B, S, t.shape[-1])

        y_conv = _conformer_conv(seq(u), conv_dw[l], conv_b[l], conv_ln_g[l], conv_ln_b[l], min(512, S))
        y_rnn = _hgrn2(seq(rq), seq(rf), seq(ri), seq(rg), lb_all[l], hgrn_norm_g[l], min(1024, S))
        wuv_pad = jnp.zeros((n_heads, dl, n_heads * dv), F32)
        for hh in range(n_heads):
            wuv_pad = wuv_pad.at[hh, :, hh * dv:(hh + 1) * dv].set(w_uv[l, hh])
        y_att = _dsa_attention(seq(aq), seq(c), seq(iq), seq(ikw), bias_near, wuv_pad.astype(BF16))
        h = _merge(h, y_conv.reshape(T, -1), y_rnn.reshape(T, -1), y_att.reshape(T, -1), gt,
                   w_conv_proj[l].astype(BF16), w_rnn_proj[l].astype(BF16), w_att_proj[l].astype(BF16),
                   w_out[l].astype(BF16), ln_mix_g[l], ln_mix_b[l], alpha, tm)
        h = _moe(h, router_w[l], router_b[l], w_expert_gu[l].astype(BF16), w_expert_dn[l].astype(BF16),
                 w_shared_gu[l].astype(BF16), w_shared_dn[l].astype(BF16), ln_ffn_g[l], ln_ffn_b[l], alpha,
                 min(1024, T))
    return h.reshape(B, S, D)
```

```python
import functools
import math

import jax
import jax.numpy as jnp
from jax import lax
from jax.experimental import pallas as pl
from jax.experimental.pallas import tpu as pltpu

N_RNN_HEADS = 4
N_IDX_HEADS = 4
IDX_HEAD_DIM = 64
TOPK_MAX = 256
N_BUCKETS = 32
MAX_EXACT = N_BUCKETS // 2
MAX_DISTANCE = 128
TOPK_EXPERTS = 4
ROUTE_SCALE = 2.5
LN_EPS = 1e-5

LANES = 128
SUBLANES = 8
VMEM_LIMIT_BYTES = 56 * 1024 * 1024

Q_BLOCK = 128
KEY_CHUNK = 4
RADIX_VARIANTS = (1, 2, 3, 4, 5, 6, 7)
HEAD_GROUP = 8
MOE_EXPERT_GROUP = 4
RNN_CHUNK = 64
RNN_SUB = 16
RNN_UNROLL = 16
NEG_BIG = -1e30
INT_MIN = -2147483648
DIGIT_BITS = 16
N_DIGITS = 32 // DIGIT_BITS
DIGIT = jnp.int16
DIGIT_MIN = -(1 << (DIGIT_BITS - 1))
assert N_DIGITS == 2
LOG2E = math.log2(math.e)

BF16 = jnp.bfloat16
F32 = jnp.float32
I16 = jnp.int16
I32 = jnp.int32


def _cparams(*sem):
    return pltpu.CompilerParams(dimension_semantics=sem, vmem_limit_bytes=VMEM_LIMIT_BYTES)


def _const_spec(shape):
    nd = len(shape)
    return pl.BlockSpec(shape, lambda *_: (0,) * nd, pipeline_mode=pl.Buffered(1))


def _ln_rows(x, g, b):
    mu = jnp.mean(x, axis=-1, keepdims=True)
    xc = x - mu
    var = jnp.mean(xc * xc, axis=-1, keepdims=True)
    return xc * lax.rsqrt(var + LN_EPS) * g + b


def _sigmoid(x):
    return 1.0 / (1.0 + jnp.exp(-x))


def _dot(a, b):
    return jnp.dot(a, b, preferred_element_type=F32)


def _dot_nt(a, b):
    return lax.dot_general(a, b, (((1,), (1,)), ((), ())), preferred_element_type=F32)


def _dot_tn(a, b):
    return lax.dot_general(a, b, (((0,), (0,)), ((), ())), preferred_element_type=F32)


def _ln_kernel(x_ref, g_ref, b_ref, o_ref):
    o_ref[...] = _ln_rows(x_ref[...], g_ref[...], b_ref[...])


def _layer_norm(x, g, b, tm):
    T, D = x.shape
    return pl.pallas_call(
        _ln_kernel,
        grid=(T // tm,),
        in_specs=[pl.BlockSpec((tm, D), lambda i: (i, 0)), _const_spec((1, D)), _const_spec((1, D))],
        out_specs=pl.BlockSpec((tm, D), lambda i: (i, 0)),
        out_shape=jax.ShapeDtypeStruct((T, D), F32),
        compiler_params=_cparams("parallel"),
        name="ln_in",
    )(x, g.reshape(1, D), b.reshape(1, D))


def _proj_kernel(dims, h_ref, w_ref, b_ref, kvg_ref,
                 u_ref, rq_ref, rf_ref, ri_ref, rg_ref, aq_ref, c_ref, iq_ref, ikw_ref, gt_ref):
    dc, dr, daq, dl, diq, dg = dims
    x = h_ref[...].astype(BF16)
    off = [0]

    def seg(n):
        o = off[0]
        off[0] = o + n
        return _dot(x, w_ref[:, o:o + n]) + b_ref[:, o:o + n]

    a = seg(dc)
    gate = seg(dc)
    u_ref[...] = (a * _sigmoid(gate)).astype(BF16)
    rq_ref[...] = seg(dr).astype(BF16)
    rf_ref[...] = seg(dr)
    ri_ref[...] = seg(dr).astype(BF16)
    rg_ref[...] = _sigmoid(seg(dr)).astype(BF16)
    scale = dl ** -0.5 * LOG2E
    for j in range(daq // 512):
        aq_ref[:, j * 512:(j + 1) * 512] = (seg(512) * scale).astype(BF16)
    c = seg(dl)
    c = c * lax.rsqrt(jnp.mean(c * c, axis=-1, keepdims=True) + LN_EPS) * kvg_ref[...]
    c_ref[...] = c.astype(BF16)
    iq_ref[...] = seg(diq).astype(BF16)
    ikw_ref[...] = seg(LANES)
    for j in range(dg // 512):
        gt_ref[:, j * 512:(j + 1) * 512] = _sigmoid(seg(512)).astype(BF16)


def _in_projection(h, w, b, kv_g, dims, tm):
    T, D = h.shape
    dc, dr, daq, dl, diq, dg = dims
    n_pad = w.shape[1]
    widths = [(dc, BF16), (dr, BF16), (dr, F32), (dr, BF16), (dr, BF16), (daq, BF16), (dl, BF16),
              (diq, BF16), (LANES, F32), (dg, BF16)]
    return pl.pallas_call(
        functools.partial(_proj_kernel, dims),
        grid=(T // tm,),
        in_specs=[pl.BlockSpec((tm, D), lambda i: (i, 0)), _const_spec((D, n_pad)),
                  _const_spec((1, n_pad)), _const_spec((1, dl))],
        out_specs=[pl.BlockSpec((tm, n), lambda i: (i, 0)) for n, _ in widths],
        out_shape=[jax.ShapeDtypeStruct((T, n), dt) for n, dt in widths],
        compiler_params=_cparams("parallel"),
        name="in_proj",
    )(h, w, b, kv_g.reshape(1, dl))


def _conv_kernel(width, ts, sub, u_ref, halo_ref, dw_ref, db_ref, g_ref, b_ref, o_ref, buf_ref):
    i = pl.program_id(1)
    halo = halo_ref[0].astype(F32)
    buf_ref[0, 0:32, :] = jnp.where(i > 0, halo, 0.0)
    buf_ref[0, 32:32 + ts, :] = u_ref[0].astype(F32)
    for r in range(1, SUBLANES):
        buf_ref[r, 0:ts + 32 - SUBLANES, :] = buf_ref[0, r:r + ts + 32 - SUBLANES, :]
    base = 32 - (width - 1)
    for s in range(ts // sub):
        acc = jnp.zeros((sub, u_ref.shape[2]), F32) + db_ref[...]
        for j in range(width):
            r = (base + j) % SUBLANES
            r0 = base + j - r + s * sub
            acc = acc + buf_ref[r, r0:r0 + sub, :] * dw_ref[j:j + 1, :]
        y = _ln_rows(acc, g_ref[...], b_ref[...])
        o_ref[0, s * sub:(s + 1) * sub, :] = (y * _sigmoid(y)).astype(BF16)


def _conformer_conv(u, dw, db, g, b, ts):
    B, S, C = u.shape
    width = dw.shape[0]
    assert width - 1 <= 32
    sub = min(64, ts)
    hb = ts // 32
    return pl.pallas_call(
        functools.partial(_conv_kernel, width, ts, sub),
        grid=(B, S // ts),
        in_specs=[pl.BlockSpec((1, ts, C), lambda bi, i: (bi, i, 0)),
                  pl.BlockSpec((1, 32, C), lambda bi, i: (bi, jnp.maximum(i * hb - 1, 0), 0)),
                  _const_spec((width, C)), _const_spec((1, C)), _const_spec((1, C)), _const_spec((1, C))],
        out_specs=pl.BlockSpec((1, ts, C), lambda bi, i: (bi, i, 0)),
        out_shape=jax.ShapeDtypeStruct((B, S, C), BF16),
        scratch_shapes=[pltpu.VMEM((SUBLANES, ts + 32, C), F32)],
        compiler_params=_cparams("parallel", "arbitrary"),
        name="conv",
    )(u, u, dw, db.reshape(1, C), g.reshape(1, C), b.reshape(1, C))


def _hgrn_chunk(q, z, v, lb, st, skew_rhs):
    C, Dh = q.shape
    ls = jnp.minimum(z, 0.0) - jnp.log(1.0 + jnp.exp(-jnp.abs(z)))
    a_ = jnp.log(lb)
    b_ = jnp.log1p(-lb) + ls
    lf = jnp.maximum(a_, b_) + jnp.log(1.0 + jnp.exp(-jnp.abs(a_ - b_)))
    kk = (1.0 - lb) * _sigmoid(-z)

    row = lax.broadcasted_iota(I32, (C, C), 0)
    col = lax.broadcasted_iota(I32, (C, C), 1)
    tri = (col <= row).astype(BF16)
    lf_hi = lf.astype(BF16)
    lf_r = lf - lf_hi.astype(F32)
    lf_mid = lf_r.astype(BF16)
    lf_lo = (lf_r - lf_mid.astype(F32)).astype(BF16)
    A = (_dot(tri, lf_hi) + (_dot(tri, lf_mid) + _dot(tri, lf_lo))) * LOG2E

    o = _dot_nt((q * jnp.exp2(A)).astype(BF16), st.astype(BF16))

    rmod = lax.broadcasted_iota(I32, (C, Dh), 0) % RNN_SUB
    G = A - jnp.log2(kk)
    parts = []
    for d in range(RNN_SUB):
        if d == 0:
            e = q * kk
        else:
            e = jnp.exp2(jnp.where(rmod >= d, A - pltpu.roll(G, d, axis=0), NEG_BIG)) * q
        parts.append(e.astype(BF16))
    z = _dot(jnp.concatenate(parts, axis=1), skew_rhs)
    P = pltpu.roll(z, 0, 1, stride=1, stride_axis=0)[:, :C]

    blocks = [jnp.zeros((RNN_SUB, C), F32)]
    for i in range(1, C // RNN_SUB):
        r0 = i * RNN_SUB
        a_i = A[r0 - 1:r0, :]
        qi = q[r0:r0 + RNN_SUB] * jnp.exp2(A[r0:r0 + RNN_SUB] - a_i)
        ki = (kk[:r0] * jnp.exp2(a_i - A[:r0])).astype(BF16)
        ki = jnp.concatenate([ki, jnp.zeros((C - r0, Dh), BF16)], axis=0)
        blocks.append(_dot_nt(qi.astype(BF16), ki))
    P = P + jnp.concatenate(blocks, axis=0)
    o = o + _dot(P.astype(BF16), v.astype(BF16))

    a_last = A[C - 1:C, :]
    kd = kk * jnp.exp2(a_last - A)
    st_new = st * jnp.exp2(a_last) + _dot_tn(v.astype(BF16), kd.astype(BF16))
    return o, st_new


def _hgrn_kernel(n_chunks, q_ref, z_ref, v_ref, g_ref, lb_ref, ng_ref, skew_ref, o_ref, st_ref):
    @pl.when(pl.program_id(2) == 0)
    def _():
        st_ref[...] = jnp.zeros_like(st_ref)

    C = RNN_CHUNK

    def chunk_body(c, st):
        sl = pl.ds(pl.multiple_of(c * C, C), C)
        o, st = _hgrn_chunk(q_ref[0, sl, :].astype(F32), z_ref[0, sl, :], v_ref[0, sl, :].astype(F32),
                            lb_ref[...], st, skew_ref[...])
        o = o * lax.rsqrt(jnp.mean(o * o, axis=-1, keepdims=True) + LN_EPS) * ng_ref[...]
        o_ref[0, sl, :] = (o * g_ref[0, sl, :].astype(F32)).astype(BF16)
        return st

    st_ref[...] = lax.fori_loop(0, n_chunks, chunk_body, st_ref[...], unroll=min(RNN_UNROLL, n_chunks))


def _hgrn2(rq, rf, ri, rg, lb, norm_g, rows):
    B, S, DR = rq.shape
    H = N_RNN_HEADS
    Dh = DR // H
    blk = pl.BlockSpec((1, rows, Dh), lambda b, h, i: (b, i, h))
    vec = pl.BlockSpec((1, Dh), lambda b, h, i: (0, h))
    assert RNN_CHUNK + RNN_SUB <= LANES
    off = jnp.arange(RNN_SUB * Dh)[:, None] // Dh
    skew_rhs = (jnp.arange(LANES)[None, :] == (LANES - off) % LANES).astype(BF16)
    return pl.pallas_call(
        functools.partial(_hgrn_kernel, rows // RNN_CHUNK),
        grid=(B, H, S // rows),
        in_specs=[blk, blk, blk, blk, vec, vec, _const_spec(skew_rhs.shape)],
        out_specs=blk,
        out_shape=jax.ShapeDtypeStruct((B, S, DR), BF16),
        scratch_shapes=[pltpu.VMEM((Dh, Dh), F32)],
        compiler_params=_cparams("parallel", "parallel", "arbitrary"),
        name="hgrn2",
    )(rq, rf, ri, rg, lb.reshape(1, DR), norm_g.reshape(1, DR), skew_rhs)


def _dsa_kernel(topk, n_heads, aq_ref, c_ref, iq_ref, ik_ref, ikw_q_ref, bias_ref, wuv_ref, o_ref,
                cpad_ref, key_ref, dig_ref, mask_ref, m_ref, l_ref, acc_ref, thr_ref, need_ref):
    QB, KC = Q_BLOCK, KEY_CHUNK
    PAD = KC - 1
    qi = pl.program_id(1)
    n_chunks = qi // KC + 1
    n_total_chunks = key_ref.shape[0] // KC
    row = lax.broadcasted_iota(I32, (QB, QB), 0)
    col = lax.broadcasted_iota(I32, (QB, QB), 1)
    dl = c_ref.shape[2]

    @pl.when(qi == 0)
    def _():
        cpad_ref[0:PAD * QB, :] = jnp.zeros((PAD * QB, dl), BF16)
        cpad_ref[PAD * QB:, :] = c_ref[0]

    iq = iq_ref[0]
    q_st = jnp.concatenate([iq[:, h * IDX_HEAD_DIM:(h + 1) * IDX_HEAD_DIM] for h in range(N_IDX_HEADS)], axis=0)
    w_rows = ikw_q_ref[0].T[IDX_HEAD_DIM:IDX_HEAD_DIM + 8, :]
    kpos_minus_q = row - col

    TRIP = 2 if n_total_chunks % 2 == 0 else 1
    n_trips = (n_chunks + TRIP - 1) // TRIP

    def score_chunk(ci):
        rows = pl.ds(pl.multiple_of(ci * (KC * QB), KC * QB), KC * QB)
        kj = ik_ref[0, rows, :][:, :IDX_HEAD_DIM].astype(BF16)
        s4 = _dot_nt(kj, q_st)
        for b in range(KC):
            j = ci * KC + b
            sc = jnp.zeros((QB, QB), F32)
            for h in range(N_IDX_HEADS):
                sc = sc + w_rows[h:h + 1, :] * jnp.maximum(s4[b * QB:(b + 1) * QB, h * QB:(h + 1) * QB], 0.0)
            sc = sc + 0.0
            bits = pltpu.bitcast(sc, I32)
            key = bits ^ ((bits >> 31) & 0x7FFFFFFF)
            key = jnp.where(kpos_minus_q <= (qi - j) * QB, key, INT_MIN)
            key_ref[j] = key
            for k in range(N_DIGITS):
                d = key >> (DIGIT_BITS * k)
                if k < N_DIGITS - 1:
                    d = (d & ((1 << DIGIT_BITS) - 1)) + DIGIT_MIN
                dig_ref[k, j] = d.astype(DIGIT)

    def score_body(t, carry):
        for u in range(TRIP):
            score_chunk(t * TRIP + u)
        return carry

    lax.fori_loop(0, n_trips, score_body, 0)

    def fill_body(ci, carry):
        for b in range(KC):
            for k in range(N_DIGITS):
                dig_ref[k, ci * KC + b] = jnp.full((QB, QB), DIGIT_MIN, DIGIT)
        return carry

    lax.fori_loop(n_trips * TRIP, n_total_chunks, fill_body, 0)

    def count_rows(acc):
        return jnp.sum(acc.astype(F32), axis=0, keepdims=True)

    def find_threshold(nc):
        one, zero = jnp.ones((), DIGIT), jnp.zeros((), DIGIT)

        def search_digit(k, need):
            def bit_body(it, cur):
                cand = cur + lax.shift_left(jnp.int32(1), DIGIT_BITS - 1 - it)
                cand_d = jnp.broadcast_to(cand, (QB, QB)).astype(DIGIT)
                acc = jnp.zeros((QB, QB), DIGIT)
                for j in range(nc * KC):
                    acc = acc + jnp.where(dig_ref[k, j] >= cand_d, one, zero)
                return jnp.where(count_rows(acc) >= need, cand, cur)

            return lax.fori_loop(0, DIGIT_BITS, bit_body, jnp.full((1, QB), DIGIT_MIN, I32))

        need = jnp.full((1, QB), float(topk), F32)
        thr = jnp.zeros((1, QB), I32)
        for k in reversed(range(N_DIGITS)):
            t = search_digit(k, need)
            t_d = jnp.broadcast_to(t, (QB, QB)).astype(DIGIT)
            acc = jnp.zeros((QB, QB), DIGIT)
            for j in range(nc * KC):
                x = dig_ref[k, j]
                acc = acc + jnp.where(x > t_d, one, zero)
                if k > 0:
                    dig_ref[k - 1, j] = jnp.where(x == t_d, dig_ref[k - 1, j], jnp.full((), DIGIT_MIN, DIGIT))
            need = need - count_rows(acc)
            thr = thr + (t if k == N_DIGITS - 1 else t - DIGIT_MIN) * (1 << (DIGIT_BITS * k))
        thr_ref[...] = thr
        need_ref[...] = need

    prev = 0
    for nc in sorted({min(n, n_total_chunks) for n in RADIX_VARIANTS} | {n_total_chunks}):
        pl.when((n_chunks > prev) & (n_chunks <= nc))(functools.partial(find_threshold, nc))
        prev = nc
    thr = jnp.maximum(thr_ref[...], INT_MIN + 1)
    thr_b = jnp.broadcast_to(thr, (QB, QB))
    need_b = jnp.broadcast_to(need_ref[...], (QB, QB))

    for p in range(PAD):
        mask_ref[p] = jnp.full((QB, QB), NEG_BIG, F32)
    tri = (col <= row).astype(BF16)

    def mask_body(t, taken):
        j0 = t * (TRIP * KC)
        keys = [key_ref[j0 + b] for b in range(TRIP * KC)]
        eqs = [k == thr_b for k in keys]
        cnts = [_dot(tri, jnp.where(e, 1.0, 0.0).astype(BF16)) for e in eqs]
        for b in range(TRIP * KC):
            tie = jnp.where(taken + cnts[b] <= need_b, 0.0, NEG_BIG)
            m = jnp.where(keys[b] > thr_b, 0.0, jnp.where(eqs[b], tie, NEG_BIG))
            mask_ref[PAD + j0 + b] = m.T
            taken = taken + cnts[b][QB - 1:QB, :]
        return taken

    lax.fori_loop(0, n_trips, mask_body, jnp.zeros((1, QB), F32))

    q_all = jnp.concatenate([aq_ref[0, :, h * dl:(h + 1) * dl] for h in range(n_heads)], axis=0)
    n_far = qi // KC

    def att_step(start, near):
        cch = cpad_ref[pl.ds(pl.multiple_of(start * QB, QB), KC * QB), :]
        for g in range(0, n_heads, HEAD_GROUP):
            lg = _dot_nt(q_all[g * QB:(g + HEAD_GROUP) * QB], cch)
            rows, alphas = [], []
            for i in range(HEAD_GROUP):
                h = g + i
                blks = []
                for b in range(KC):
                    x = lg[i * QB:(i + 1) * QB, b * QB:(b + 1) * QB] + mask_ref[start + b]
                    if near and b == KC - 1:
                        x = x + bias_ref[0, h]
                    if near and b == KC - 2:
                        x = x + bias_ref[1, h]
                    blks.append(x)
                m_old = m_ref[h]
                mx = blks[0]
                for b in range(1, KC):
                    mx = jnp.maximum(mx, blks[b])
                m_new = jnp.maximum(m_old, jnp.broadcast_to(jnp.max(mx, axis=1, keepdims=True), (QB, QB)))
                alpha = jnp.exp2(m_old - m_new)
                ps = [jnp.exp2(blks[b] - m_new) for b in range(KC)]
                s = ps[0]
                for b in range(1, KC):
                    s = s + ps[b]
                l_ref[h] = alpha * l_ref[h] + s
                m_ref[h] = m_new
                rows.append(jnp.concatenate([p.astype(BF16) for p in ps], axis=1))
                alphas.append(alpha)
            pv = _dot(jnp.concatenate(rows, axis=0), cch)
            for i in range(HEAD_GROUP):
                sl = slice((g + i) * QB, (g + i + 1) * QB)
                acc_ref[sl, :] = alphas[i] * acc_ref[sl, :] + pv[i * QB:(i + 1) * QB, :]

    m_ref[...] = jnp.full(m_ref.shape, NEG_BIG, F32)
    l_ref[...] = jnp.zeros(l_ref.shape, F32)
    acc_ref[...] = jnp.zeros(acc_ref.shape, F32)

    def far_body(m, carry):
        att_step(qi - KC * (m + 1), False)
        return carry

    lax.fori_loop(0, n_far, far_body, 0)
    att_step(qi, True)

    out = jnp.zeros((QB, o_ref.shape[2]), F32)
    for h in range(n_heads):
        oh = acc_ref[h * QB:(h + 1) * QB, :] / jnp.sum(l_ref[h], axis=1, keepdims=True)
        out = out + _dot(oh.astype(BF16), wuv_ref[h])
    o_ref[0] = out.astype(BF16)


def _dsa_attention(aq, c, iq, ikw, bias_near, wuv_pad):
    B, S, DQ = aq.shape
    dl = c.shape[2]
    assert dl == Q_BLOCK
    assert (DQ // dl) % HEAD_GROUP == 0
    n_heads = DQ // dl
    d_att = wuv_pad.shape[2]
    topk = min(TOPK_MAX, S // 4)
    QB, KC = Q_BLOCK, KEY_CHUNK
    nkb = S // QB
    assert nkb % KC == 0
    return pl.pallas_call(
        functools.partial(_dsa_kernel, topk, n_heads),
        grid=(B, nkb),
        in_specs=[pl.BlockSpec((1, QB, DQ), lambda b, i: (b, i, 0)),
                  pl.BlockSpec((1, S, dl), lambda b, i: (b, 0, 0)),
                  pl.BlockSpec((1, QB, iq.shape[2]), lambda b, i: (b, i, 0)),
                  pl.BlockSpec((1, S, LANES), lambda b, i: (b, 0, 0)),
                  pl.BlockSpec((1, QB, LANES), lambda b, i: (b, i, 0)),
                  _const_spec(bias_near.shape), _const_spec(wuv_pad.shape)],
        out_specs=pl.BlockSpec((1, QB, d_att), lambda b, i: (b, i, 0)),
        out_shape=jax.ShapeDtypeStruct((B, S, d_att), BF16),
        scratch_shapes=[pltpu.VMEM(((nkb + KC - 1) * QB, dl), BF16),
                        pltpu.VMEM((nkb, QB, QB), I32), pltpu.VMEM((N_DIGITS, nkb, QB, QB), DIGIT),
                        pltpu.VMEM((nkb + KC - 1, QB, QB), F32),
                        pltpu.VMEM((n_heads, QB, QB), F32), pltpu.VMEM((n_heads, QB, QB), F32),
                        pltpu.VMEM((n_heads * QB, dl), F32), pltpu.VMEM((1, QB), I32),
                        pltpu.VMEM((1, QB), F32)],
        compiler_params=_cparams("parallel", "arbitrary"),
        name="dsa",
    )(aq, c, iq, ikw, ikw, bias_near, wuv_pad)


def _t5_bucket(n):
    nf = jnp.maximum(n, 1).astype(F32)
    large = MAX_EXACT + (jnp.log(nf / MAX_EXACT) / math.log(MAX_DISTANCE / MAX_EXACT)
                         * (N_BUCKETS - MAX_EXACT)).astype(I32)
    large = jnp.minimum(large, N_BUCKETS - 1)
    return jnp.where(n < MAX_EXACT, n, large)


def _near_bias(rel_bias):
    QB = Q_BLOCK
    assert QB >= MAX_DISTANCE
    r = jnp.arange(QB)[:, None] - jnp.arange(QB)[None, :]
    def lookup(bucket):
        one_hot = jax.nn.one_hot(bucket, N_BUCKETS, dtype=F32)
        return jnp.einsum('rcb,bh->rch', one_hot, rel_bias.astype(F32), precision=lax.Precision.HIGHEST)

    far = rel_bias[N_BUCKETS - 1]
    diag = lookup(_t5_bucket(jnp.maximum(r, 0))) - far
    prev = lookup(_t5_bucket(r + QB)) - far
    return (jnp.stack([diag, prev]).transpose(0, 3, 1, 2) * LOG2E).astype(F32)


def _merge_kernel(alpha, D, h_ref, u_ref, r_ref, a_ref, gt_ref, wc_ref, wr_ref, wa_ref, wo_ref, g_ref, b_ref,
                  o_ref):
    y = gt_ref[:, 0:D].astype(F32) * _dot(u_ref[...], wc_ref[...])
    y = y + gt_ref[:, D:2 * D].astype(F32) * _dot(r_ref[...], wr_ref[...])
    y = y + gt_ref[:, 2 * D:3 * D].astype(F32) * _dot(a_ref[...], wa_ref[...])
    z = alpha * h_ref[...] + _dot(y.astype(BF16), wo_ref[...])
    o_ref[...] = _ln_rows(z, g_ref[...], b_ref[...])


def _merge(h, u, r, a, gt, wc, wr, wa, wo, g, b, alpha, tm):
    T, D = h.shape
    row = lambda n: pl.BlockSpec((tm, n), lambda i: (i, 0))
    return pl.pallas_call(
        functools.partial(_merge_kernel, alpha, D),
        grid=(T // tm,),
        in_specs=[row(D), row(u.shape[1]), row(r.shape[1]), row(a.shape[1]), row(3 * D),
                  _const_spec(wc.shape), _const_spec(wr.shape), _const_spec(wa.shape), _const_spec(wo.shape),
                  _const_spec((1, D)), _const_spec((1, D))],
        out_specs=row(D),
        out_shape=jax.ShapeDtypeStruct((T, D), F32),
        compiler_params=_cparams("parallel"),
        name="merge",
    )(h, u, r, a, gt, wc, wr, wa, wo, g.reshape(1, D), b.reshape(1, D))


def _split_bf16(x):
    hi = x.astype(BF16)
    return hi, (x - hi.astype(F32)).astype(BF16)


def _moe_kernel(alpha, n_exp, de, h_ref, rw_hi_ref, rw_lo_ref, rb_ref, wgu_ref, wdn_ref, wsgu_ref, wsdn_ref,
                g_ref, b_ref, o_ref, xb_ref, gate_ref, acc_ref):
    e = pl.program_id(1)
    lane = lax.broadcasted_iota(I32, gate_ref.shape, 1)

    @pl.when(e == 0)
    def _():
        x = h_ref[...]
        x_hi, x_lo = _split_bf16(x)
        xb_ref[...] = x_hi
        logits = _dot(x_hi, rw_hi_ref[...]) + (_dot(x_lo, rw_hi_ref[...]) + _dot(x_hi, rw_lo_ref[...]))
        scores = _sigmoid(logits)
        sb = scores + rb_ref[...]
        sel = jnp.zeros(sb.shape, jnp.bool_)
        for _ in range(TOPK_EXPERTS):
            m = jnp.max(sb, axis=-1, keepdims=True)
            first = jnp.min(jnp.where(sb == m, lane, n_exp), axis=-1, keepdims=True)
            hit = lane == first
            sel = sel | hit
            sb = jnp.where(hit, -jnp.inf, sb)
        w = jnp.where(sel, scores, 0.0)
        gate_ref[...] = w / jnp.sum(w, axis=-1, keepdims=True) * ROUTE_SCALE
        ds = wsdn_ref.shape[0]
        s = _dot(x_hi, wsgu_ref[...])
        a_s, u_s = s[:, :ds], s[:, ds:]
        acc_ref[...] = _dot((a_s * _sigmoid(a_s) * u_s).astype(BF16), wsdn_ref[...])

    xb = xb_ref[...]
    n_grp = wgu_ref.shape[0]
    hmids = []
    for i in range(n_grp):
        au = _dot(xb, wgu_ref[i])
        a, u = au[:, :de], au[:, de:]
        ge = jnp.sum(jnp.where(lane == e * n_grp + i, gate_ref[...], 0.0), axis=-1, keepdims=True)
        hmids.append((a * _sigmoid(a) * u * ge).astype(BF16))
    acc_ref[...] += _dot(jnp.concatenate(hmids, axis=1), wdn_ref[...].reshape(n_grp * de, -1))

    @pl.when(e == n_exp // n_grp - 1)
    def _():
        o_ref[...] = _ln_rows(alpha * h_ref[...] + acc_ref[...], g_ref[...], b_ref[...])


def _moe(h, rw, rb, wgu, wdn, wsgu, wsdn, g, b, alpha, tm):
    T, D = h.shape
    n_exp, _, de2 = wgu.shape
    de = de2 // 2
    rw_hi, rw_lo = _split_bf16(rw)
    return pl.pallas_call(
        functools.partial(_moe_kernel, alpha, n_exp, de),
        grid=(T // tm, n_exp // MOE_EXPERT_GROUP),
        in_specs=[pl.BlockSpec((tm, D), lambda i, e: (i, 0)),
                  _const_spec(rw.shape), _const_spec(rw.shape), _const_spec((1, n_exp)),
                  pl.BlockSpec((MOE_EXPERT_GROUP, D, de2), lambda i, e: (e, 0, 0)),
                  pl.BlockSpec((MOE_EXPERT_GROUP, de, D), lambda i, e: (e, 0, 0)),
                  _const_spec(wsgu.shape), _const_spec(wsdn.shape),
                  _const_spec((1, D)), _const_spec((1, D))],
        out_specs=pl.BlockSpec((tm, D), lambda i, e: (i, 0)),
        out_shape=jax.ShapeDtypeStruct((T, D), F32),
        scratch_shapes=[pltpu.VMEM((tm, D), BF16), pltpu.VMEM((tm, n_exp), F32), pltpu.VMEM((tm, D), F32)],
        compiler_params=_cparams("parallel", "arbitrary"),
        name="moe",
    )(h, rw_hi, rw_lo, rb.reshape(1, n_exp), wgu, wdn, wsgu, wsdn, g.reshape(1, D), b.reshape(1, D))


def _pack_in_weights(w, b, dims):
    dc, dr, daq, dl, diq, dg = dims
    aligned = [2 * dc, dr, dr, dr, dr, daq, dl, diq]
    assert all(s % LANES == 0 for s in aligned) and IDX_HEAD_DIM + N_IDX_HEADS <= LANES
    split = sum(aligned) + IDX_HEAD_DIM + N_IDX_HEADS
    assert split + dg == w.shape[1]
    pad = LANES - IDX_HEAD_DIM - N_IDX_HEADS
    wb = w.astype(BF16)
    w_pack = jnp.concatenate([wb[:, :split], jnp.zeros((w.shape[0], pad), BF16), wb[:, split:]], axis=1)
    b_pack = jnp.concatenate([b[:split], jnp.zeros((pad,), b.dtype), b[split:]])
    return w_pack, b_pack.reshape(1, -1).astype(F32)


def kernel(x, ln_in_g, ln_in_b, w_in, b_in, conv_dw, conv_b, conv_ln_g, conv_ln_b, w_conv_proj, hgrn_gamma,
           hgrn_norm_g, w_rnn_proj, kv_norm_g, w_uv, w_att_proj, rel_bias, w_out, ln_mix_g, ln_mix_b,
           router_w, router_b, w_expert_gu, w_expert_dn, w_shared_gu, w_shared_dn, ln_ffn_g, ln_ffn_b):
    B, S, D = x.shape
    T = B * S
    depth = w_in.shape[0]
    dc = conv_dw.shape[2]
    dr = hgrn_gamma.shape[1]
    n_heads, dl, dv = w_uv.shape[1:]
    daq = n_heads * dl
    diq = N_IDX_HEADS * IDX_HEAD_DIM
    dims = (dc, dr, daq, dl, diq, 3 * D)
    alpha = (2 * depth) ** 0.25

    tm = min(512, T)
    lb_all = jnp.cumsum(jax.nn.softmax(hgrn_gamma.astype(F32), axis=0), axis=0)
    lb_all = lb_all - lb_all[0]
    bias_near = _near_bias(rel_bias)

    h = _layer_norm(x.reshape(T, D), ln_in_g, ln_in_b, tm)
    for l in range(depth):
        w_pack, b_pack = _pack_in_weights(w_in[l], b_in[l], dims)
        u, rq, rf, ri, rg, aq, c, iq, ikw, gt = _in_projection(h, w_pack, b_pack, kv_norm_g[l], dims, tm)

        def seq(t):
            return t.reshape(B, S, t.shape[-1])

        y_conv = _conformer_conv(seq(u), conv_dw[l], conv_b[l], conv_ln_g[l], conv_ln_b[l], min(512, S))
        y_rnn = _hgrn2(seq(rq), seq(rf), seq(ri), seq(rg), lb_all[l], hgrn_norm_g[l], min(1024, S))
        wuv_pad = jnp.zeros((n_heads, dl, n_heads * dv), F32)
        for hh in range(n_heads):
            wuv_pad = wuv_pad.at[hh, :, hh * dv:(hh + 1) * dv].set(w_uv[l, hh])
        y_att = _dsa_attention(seq(aq), seq(c), seq(iq), seq(ikw), bias_near, wuv_pad.astype(BF16))
        h = _merge(h, y_conv.reshape(T, -1), y_rnn.reshape(T, -1), y_att.reshape(T, -1), gt,
                   w_conv_proj[l].astype(BF16), w_rnn_proj[l].astype(BF16), w_att_proj[l].astype(BF16),
                   w_out[l].astype(BF16), ln_mix_g[l], ln_mix_b[l], alpha, tm)
        h = _moe(h, router_w[l], router_b[l], w_expert_gu[l].astype(BF16), w_expert_dn[l].astype(BF16),
                 w_shared_gu[l].astype(BF16), w_shared_dn[l].astype(BF16), ln_ffn_g[l], ln_ffn_b[l], alpha,
                 min(1024, T))
    return h.reshape(B, S, D)
```

```python
import functools
import math

import jax
import jax.numpy as jnp
from jax import lax
from jax.experimental import pallas as pl
from jax.experimental.pallas import tpu as pltpu

N_RNN_HEADS = 4
N_IDX_HEADS = 4
IDX_HEAD_DIM = 64
TOPK_MAX = 256
N_BUCKETS = 32
MAX_EXACT = N_BUCKETS // 2
MAX_DISTANCE = 128
TOPK_EXPERTS = 4
ROUTE_SCALE = 2.5
LN_EPS = 1e-5

LANES = 128
SUBLANES = 8
VMEM_LIMIT_BYTES = 56 * 1024 * 1024

Q_BLOCK = 128
KEY_CHUNK = 4
RADIX_VARIANTS = (1, 2, 3, 4, 5, 6, 7)
PROJ_COLS = 512
CONV_HALO = 32
CONV_ROWS = 64
TILE_ROWS = dict(ln=512, in_proj=512, conv=512, hgrn2=1024, merge=512, moe=1024)
HEAD_GROUP = 8
MOE_EXPERT_GROUP = 4
RNN_CHUNK = 64
RNN_SUB = 16
RNN_UNROLL = 16
NEG_BIG = -1e30
INT_MIN = -2147483648
DIGIT_BITS = 16
N_DIGITS = 32 // DIGIT_BITS
DIGIT = jnp.int16
DIGIT_MIN = -(1 << (DIGIT_BITS - 1))
assert N_DIGITS == 2
LOG2E = math.log2(math.e)

BF16 = jnp.bfloat16
F32 = jnp.float32
I32 = jnp.int32


def _cparams(*sem):
    return pltpu.CompilerParams(dimension_semantics=sem, vmem_limit_bytes=VMEM_LIMIT_BYTES)


def _const_spec(shape):
    nd = len(shape)
    return pl.BlockSpec(shape, lambda *_: (0,) * nd, pipeline_mode=pl.Buffered(1))


def _ln_rows(x, g, b):
    mu = jnp.mean(x, axis=-1, keepdims=True)
    xc = x - mu
    var = jnp.mean(xc * xc, axis=-1, keepdims=True)
    return xc * lax.rsqrt(var + LN_EPS) * g + b


def _sigmoid(x):
    return 1.0 / (1.0 + jnp.exp(-x))


def _dot(a, b):
    return jnp.dot(a, b, preferred_element_type=F32)


def _dot_nt(a, b):
    return lax.dot_general(a, b, (((1,), (1,)), ((), ())), preferred_element_type=F32)


def _dot_tn(a, b):
    return lax.dot_general(a, b, (((0,), (0,)), ((), ())), preferred_element_type=F32)


def _ln_kernel(x_ref, g_ref, b_ref, o_ref):
    o_ref[...] = _ln_rows(x_ref[...], g_ref[...], b_ref[...])


def _layer_norm(x, g, b, tm):
    T, D = x.shape
    return pl.pallas_call(
        _ln_kernel,
        grid=(T // tm,),
        in_specs=[pl.BlockSpec((tm, D), lambda i: (i, 0)), _const_spec((1, D)), _const_spec((1, D))],
        out_specs=pl.BlockSpec((tm, D), lambda i: (i, 0)),
        out_shape=jax.ShapeDtypeStruct((T, D), F32),
        compiler_params=_cparams("parallel"),
        name="ln_in",
    )(x, g.reshape(1, D), b.reshape(1, D))


def _proj_kernel(dims, h_ref, w_ref, b_ref, kvg_ref,
                 u_ref, rq_ref, rf_ref, ri_ref, rg_ref, aq_ref, c_ref, iq_ref, ikw_ref, gt_ref):
    dc, dr, daq, dl, diq, dg = dims
    x = h_ref[...].astype(BF16)
    off = [0]

    def seg(n):
        o = off[0]
        off[0] = o + n
        return _dot(x, w_ref[:, o:o + n]) + b_ref[:, o:o + n]

    a = seg(dc)
    gate = seg(dc)
    u_ref[...] = (a * _sigmoid(gate)).astype(BF16)
    rq_ref[...] = seg(dr).astype(BF16)
    rf_ref[...] = seg(dr)
    ri_ref[...] = seg(dr).astype(BF16)
    rg_ref[...] = _sigmoid(seg(dr)).astype(BF16)
    scale = dl ** -0.5 * LOG2E
    pc = PROJ_COLS
    for j in range(daq // pc):
        aq_ref[:, j * pc:(j + 1) * pc] = (seg(pc) * scale).astype(BF16)
    c = seg(dl)
    c = c * lax.rsqrt(jnp.mean(c * c, axis=-1, keepdims=True) + LN_EPS) * kvg_ref[...]
    c_ref[...] = c.astype(BF16)
    iq_ref[...] = seg(diq).astype(BF16)
    ikw_ref[...] = seg(LANES)
    for j in range(dg // pc):
        gt_ref[:, j * pc:(j + 1) * pc] = _sigmoid(seg(pc)).astype(BF16)


def _in_projection(h, w, b, kv_g, dims, tm):
    T, D = h.shape
    dc, dr, daq, dl, diq, dg = dims
    n_pad = w.shape[1]
    widths = [(dc, BF16), (dr, BF16), (dr, F32), (dr, BF16), (dr, BF16), (daq, BF16), (dl, BF16),
              (diq, BF16), (LANES, F32), (dg, BF16)]
    return pl.pallas_call(
        functools.partial(_proj_kernel, dims),
        grid=(T // tm,),
        in_specs=[pl.BlockSpec((tm, D), lambda i: (i, 0)), _const_spec((D, n_pad)),
                  _const_spec((1, n_pad)), _const_spec((1, dl))],
        out_specs=[pl.BlockSpec((tm, n), lambda i: (i, 0)) for n, _ in widths],
        out_shape=[jax.ShapeDtypeStruct((T, n), dt) for n, dt in widths],
        compiler_params=_cparams("parallel"),
        name="in_proj",
    )(h, w, b, kv_g.reshape(1, dl))


def _conv_kernel(width, ts, sub, u_ref, halo_ref, dw_ref, db_ref, g_ref, b_ref, o_ref, buf_ref):
    i = pl.program_id(1)
    halo = halo_ref[0].astype(F32)
    H = CONV_HALO
    buf_ref[0, 0:H, :] = jnp.where(i > 0, halo, 0.0)
    buf_ref[0, H:H + ts, :] = u_ref[0].astype(F32)
    for r in range(1, SUBLANES):
        buf_ref[r, 0:ts + H - SUBLANES, :] = buf_ref[0, r:r + ts + H - SUBLANES, :]
    base = H - (width - 1)
    for s in range(ts // sub):
        acc = jnp.zeros((sub, u_ref.shape[2]), F32) + db_ref[...]
        for j in range(width):
            r = (base + j) % SUBLANES
            r0 = base + j - r + s * sub
            acc = acc + buf_ref[r, r0:r0 + sub, :] * dw_ref[j:j + 1, :]
        y = _ln_rows(acc, g_ref[...], b_ref[...])
        o_ref[0, s * sub:(s + 1) * sub, :] = (y * _sigmoid(y)).astype(BF16)


def _conformer_conv(u, dw, db, g, b, ts):
    B, S, C = u.shape
    width = dw.shape[0]
    assert width - 1 <= CONV_HALO and ts % CONV_HALO == 0
    sub = min(CONV_ROWS, ts)
    hb = ts // CONV_HALO
    return pl.pallas_call(
        functools.partial(_conv_kernel, width, ts, sub),
        grid=(B, S // ts),
        in_specs=[pl.BlockSpec((1, ts, C), lambda bi, i: (bi, i, 0)),
                  pl.BlockSpec((1, CONV_HALO, C), lambda bi, i: (bi, jnp.maximum(i * hb - 1, 0), 0)),
                  _const_spec((width, C)), _const_spec((1, C)), _const_spec((1, C)), _const_spec((1, C))],
        out_specs=pl.BlockSpec((1, ts, C), lambda bi, i: (bi, i, 0)),
        out_shape=jax.ShapeDtypeStruct((B, S, C), BF16),
        scratch_shapes=[pltpu.VMEM((SUBLANES, ts + CONV_HALO, C), F32)],
        compiler_params=_cparams("parallel", "arbitrary"),
        name="conv",
    )(u, u, dw, db.reshape(1, C), g.reshape(1, C), b.reshape(1, C))


def _hgrn_chunk(q, z, v, lb, st, skew_rhs):
    C, Dh = q.shape
    ls = jnp.minimum(z, 0.0) - jnp.log(1.0 + jnp.exp(-jnp.abs(z)))
    a_ = jnp.log(lb)
    b_ = jnp.log1p(-lb) + ls
    lf = jnp.maximum(a_, b_) + jnp.log(1.0 + jnp.exp(-jnp.abs(a_ - b_)))
    kk = (1.0 - lb) * _sigmoid(-z)

    row = lax.broadcasted_iota(I32, (C, C), 0)
    col = lax.broadcasted_iota(I32, (C, C), 1)
    tri = (col <= row).astype(BF16)
    lf_hi = lf.astype(BF16)
    lf_r = lf - lf_hi.astype(F32)
    lf_mid = lf_r.astype(BF16)
    lf_lo = (lf_r - lf_mid.astype(F32)).astype(BF16)
    A = (_dot(tri, lf_hi) + (_dot(tri, lf_mid) + _dot(tri, lf_lo))) * LOG2E

    o = _dot_nt((q * jnp.exp2(A)).astype(BF16), st.astype(BF16))

    rmod = lax.broadcasted_iota(I32, (C, Dh), 0) % RNN_SUB
    G = A - jnp.log2(kk)
    parts = []
    for d in range(RNN_SUB):
        if d == 0:
            e = q * kk
        else:
            e = jnp.exp2(jnp.where(rmod >= d, A - pltpu.roll(G, d, axis=0), NEG_BIG)) * q
        parts.append(e.astype(BF16))
    z = _dot(jnp.concatenate(parts, axis=1), skew_rhs)
    P = pltpu.roll(z, 0, 1, stride=1, stride_axis=0)[:, :C]

    blocks = [jnp.zeros((RNN_SUB, C), F32)]
    for i in range(1, C // RNN_SUB):
        r0 = i * RNN_SUB
        a_i = A[r0 - 1:r0, :]
        qi = q[r0:r0 + RNN_SUB] * jnp.exp2(A[r0:r0 + RNN_SUB] - a_i)
        ki = (kk[:r0] * jnp.exp2(a_i - A[:r0])).astype(BF16)
        ki = jnp.concatenate([ki, jnp.zeros((C - r0, Dh), BF16)], axis=0)
        blocks.append(_dot_nt(qi.astype(BF16), ki))
    P = P + jnp.concatenate(blocks, axis=0)
    o = o + _dot(P.astype(BF16), v.astype(BF16))

    a_last = A[C - 1:C, :]
    kd = kk * jnp.exp2(a_last - A)
    st_new = st * jnp.exp2(a_last) + _dot_tn(v.astype(BF16), kd.astype(BF16))
    return o, st_new


def _hgrn_kernel(n_chunks, q_ref, z_ref, v_ref, g_ref, lb_ref, ng_ref, skew_ref, o_ref, st_ref):
    @pl.when(pl.program_id(2) == 0)
    def _():
        st_ref[...] = jnp.zeros_like(st_ref)

    C = RNN_CHUNK

    def chunk_body(c, st):
        sl = pl.ds(pl.multiple_of(c * C, C), C)
        o, st = _hgrn_chunk(q_ref[0, sl, :].astype(F32), z_ref[0, sl, :], v_ref[0, sl, :].astype(F32),
                            lb_ref[...], st, skew_ref[...])
        o = o * lax.rsqrt(jnp.mean(o * o, axis=-1, keepdims=True) + LN_EPS) * ng_ref[...]
        o_ref[0, sl, :] = (o * g_ref[0, sl, :].astype(F32)).astype(BF16)
        return st

    st_ref[...] = lax.fori_loop(0, n_chunks, chunk_body, st_ref[...], unroll=min(RNN_UNROLL, n_chunks))


def _hgrn2(rq, rf, ri, rg, lb, norm_g, rows):
    B, S, DR = rq.shape
    H = N_RNN_HEADS
    Dh = DR // H
    blk = pl.BlockSpec((1, rows, Dh), lambda b, h, i: (b, i, h))
    vec = pl.BlockSpec((1, Dh), lambda b, h, i: (0, h))
    assert RNN_CHUNK + RNN_SUB <= LANES
    off = jnp.arange(RNN_SUB * Dh)[:, None] // Dh
    skew_rhs = (jnp.arange(LANES)[None, :] == (LANES - off) % LANES).astype(BF16)
    return pl.pallas_call(
        functools.partial(_hgrn_kernel, rows // RNN_CHUNK),
        grid=(B, H, S // rows),
        in_specs=[blk, blk, blk, blk, vec, vec, _const_spec(skew_rhs.shape)],
        out_specs=blk,
        out_shape=jax.ShapeDtypeStruct((B, S, DR), BF16),
        scratch_shapes=[pltpu.VMEM((Dh, Dh), F32)],
        compiler_params=_cparams("parallel", "parallel", "arbitrary"),
        name="hgrn2",
    )(rq, rf, ri, rg, lb.reshape(1, DR), norm_g.reshape(1, DR), skew_rhs)


def _dsa_kernel(topk, n_heads, aq_ref, c_ref, iq_ref, ik_ref, ikw_q_ref, bias_ref, wuv_ref, o_ref,
                cpad_ref, key_ref, dig_ref, mask_ref, m_ref, l_ref, acc_ref, thr_ref, need_ref):
    QB, KC = Q_BLOCK, KEY_CHUNK
    PAD = KC - 1
    qi = pl.program_id(1)
    n_chunks = qi // KC + 1
    n_total_chunks = key_ref.shape[0] // KC
    row = lax.broadcasted_iota(I32, (QB, QB), 0)
    col = lax.broadcasted_iota(I32, (QB, QB), 1)
    dl = c_ref.shape[2]

    @pl.when(qi == 0)
    def _():
        cpad_ref[0:PAD * QB, :] = jnp.zeros((PAD * QB, dl), BF16)
        cpad_ref[PAD * QB:, :] = c_ref[0]

    iq = iq_ref[0]
    q_st = jnp.concatenate([iq[:, h * IDX_HEAD_DIM:(h + 1) * IDX_HEAD_DIM] for h in range(N_IDX_HEADS)], axis=0)
    w_rows = ikw_q_ref[0].T[IDX_HEAD_DIM:IDX_HEAD_DIM + SUBLANES, :]
    kpos_minus_q = row - col

    TRIP = 2 if n_total_chunks % 2 == 0 else 1
    n_trips = (n_chunks + TRIP - 1) // TRIP

    def score_chunk(ci):
        rows = pl.ds(pl.multiple_of(ci * (KC * QB), KC * QB), KC * QB)
        kj = ik_ref[0, rows, :][:, :IDX_HEAD_DIM].astype(BF16)
        s4 = _dot_nt(kj, q_st)
        for b in range(KC):
            j = ci * KC + b
            sc = jnp.zeros((QB, QB), F32)
            for h in range(N_IDX_HEADS):
                sc = sc + w_rows[h:h + 1, :] * jnp.maximum(s4[b * QB:(b + 1) * QB, h * QB:(h + 1) * QB], 0.0)
            sc = sc + 0.0
            bits = pltpu.bitcast(sc, I32)
            key = bits ^ ((bits >> 31) & 0x7FFFFFFF)
            key = jnp.where(kpos_minus_q <= (qi - j) * QB, key, INT_MIN)
            key_ref[j] = key
            for k in range(N_DIGITS):
                d = key >> (DIGIT_BITS * k)
                if k < N_DIGITS - 1:
                    d = (d & ((1 << DIGIT_BITS) - 1)) + DIGIT_MIN
                dig_ref[k, j] = d.astype(DIGIT)

    def score_body(t, carry):
        for u in range(TRIP):
            score_chunk(t * TRIP + u)
        return carry

    lax.fori_loop(0, n_trips, score_body, 0)

    def fill_body(ci, carry):
        for b in range(KC):
            for k in range(N_DIGITS):
                dig_ref[k, ci * KC + b] = jnp.full((QB, QB), DIGIT_MIN, DIGIT)
        return carry

    lax.fori_loop(n_trips * TRIP, n_total_chunks, fill_body, 0)

    def count_rows(acc):
        return jnp.sum(acc.astype(F32), axis=0, keepdims=True)

    def find_threshold(nc):
        one, zero = jnp.ones((), DIGIT), jnp.zeros((), DIGIT)

        def search_digit(k, need):
            def bit_body(it, cur):
                cand = cur + lax.shift_left(jnp.int32(1), DIGIT_BITS - 1 - it)
                cand_d = jnp.broadcast_to(cand, (QB, QB)).astype(DIGIT)
                acc = jnp.zeros((QB, QB), DIGIT)
                for j in range(nc * KC):
                    acc = acc + jnp.where(dig_ref[k, j] >= cand_d, one, zero)
                return jnp.where(count_rows(acc) >= need, cand, cur)

            return lax.fori_loop(0, DIGIT_BITS, bit_body, jnp.full((1, QB), DIGIT_MIN, I32))

        need = jnp.full((1, QB), float(topk), F32)
        thr = jnp.zeros((1, QB), I32)
        for k in reversed(range(N_DIGITS)):
            t = search_digit(k, need)
            t_d = jnp.broadcast_to(t, (QB, QB)).astype(DIGIT)
            acc = jnp.zeros((QB, QB), DIGIT)
            for j in range(nc * KC):
                x = dig_ref[k, j]
                acc = acc + jnp.where(x > t_d, one, zero)
                if k > 0:
                    dig_ref[k - 1, j] = jnp.where(x == t_d, dig_ref[k - 1, j], jnp.full((), DIGIT_MIN, DIGIT))
            need = need - count_rows(acc)
            thr = thr + (t if k == N_DIGITS - 1 else t - DIGIT_MIN) * (1 << (DIGIT_BITS * k))
        thr_ref[...] = thr
        need_ref[...] = need

    prev = 0
    for nc in sorted({min(n, n_total_chunks) for n in RADIX_VARIANTS} | {n_total_chunks}):
        pl.when((n_chunks > prev) & (n_chunks <= nc))(functools.partial(find_threshold, nc))
        prev = nc
    thr = jnp.maximum(thr_ref[...], INT_MIN + 1)
    thr_b = jnp.broadcast_to(thr, (QB, QB))
    need_b = jnp.broadcast_to(need_ref[...], (QB, QB))

    for p in range(PAD):
        mask_ref[p] = jnp.full((QB, QB), NEG_BIG, F32)
    tri = (col <= row).astype(BF16)

    def mask_body(t, taken):
        j0 = t * (TRIP * KC)
        keys = [key_ref[j0 + b] for b in range(TRIP * KC)]
        eqs = [k == thr_b for k in keys]
        cnts = [_dot(tri, jnp.where(e, 1.0, 0.0).astype(BF16)) for e in eqs]
        for b in range(TRIP * KC):
            tie = jnp.where(taken + cnts[b] <= need_b, 0.0, NEG_BIG)
            m = jnp.where(keys[b] > thr_b, 0.0, jnp.where(eqs[b], tie, NEG_BIG))
            mask_ref[PAD + j0 + b] = m.T
            taken = taken + cnts[b][QB - 1:QB, :]
        return taken

    lax.fori_loop(0, n_trips, mask_body, jnp.zeros((1, QB), F32))

    q_all = jnp.concatenate([aq_ref[0, :, h * dl:(h + 1) * dl] for h in range(n_heads)], axis=0)
    n_far = qi // KC

    def att_step(start, near):
        cch = cpad_ref[pl.ds(pl.multiple_of(start * QB, QB), KC * QB), :]
        for g in range(0, n_heads, HEAD_GROUP):
            lg = _dot_nt(q_all[g * QB:(g + HEAD_GROUP) * QB], cch)
            rows, alphas = [], []
            for i in range(HEAD_GROUP):
                h = g + i
                blks = []
                for b in range(KC):
                    x = lg[i * QB:(i + 1) * QB, b * QB:(b + 1) * QB] + mask_ref[start + b]
                    if near and b == KC - 1:
                        x = x + bias_ref[0, h]
                    if near and b == KC - 2:
                        x = x + bias_ref[1, h]
                    blks.append(x)
                m_old = m_ref[h]
                mx = blks[0]
                for b in range(1, KC):
                    mx = jnp.maximum(mx, blks[b])
                m_new = jnp.maximum(m_old, jnp.broadcast_to(jnp.max(mx, axis=1, keepdims=True), (QB, QB)))
                alpha = jnp.exp2(m_old - m_new)
                ps = [jnp.exp2(blks[b] - m_new) for b in range(KC)]
                s = ps[0]
                for b in range(1, KC):
                    s = s + ps[b]
                l_ref[h] = alpha * l_ref[h] + s
                m_ref[h] = m_new
                rows.append(jnp.concatenate([p.astype(BF16) for p in ps], axis=1))
                alphas.append(alpha)
            pv = _dot(jnp.concatenate(rows, axis=0), cch)
            for i in range(HEAD_GROUP):
                sl = slice((g + i) * QB, (g + i + 1) * QB)
                acc_ref[sl, :] = alphas[i] * acc_ref[sl, :] + pv[i * QB:(i + 1) * QB, :]

    m_ref[...] = jnp.full(m_ref.shape, NEG_BIG, F32)
    l_ref[...] = jnp.zeros(l_ref.shape, F32)
    acc_ref[...] = jnp.zeros(acc_ref.shape, F32)

    def far_body(m, carry):
        att_step(qi - KC * (m + 1), False)
        return carry

    lax.fori_loop(0, n_far, far_body, 0)
    att_step(qi, True)

    out = jnp.zeros((QB, o_ref.shape[2]), F32)
    for h in range(n_heads):
        oh = acc_ref[h * QB:(h + 1) * QB, :] / jnp.sum(l_ref[h], axis=1, keepdims=True)
        out = out + _dot(oh.astype(BF16), wuv_ref[h])
    o_ref[0] = out.astype(BF16)


def _dsa_attention(aq, c, iq, ikw, bias_near, wuv_pad):
    B, S, DQ = aq.shape
    dl = c.shape[2]
    assert dl == Q_BLOCK
    assert (DQ // dl) % HEAD_GROUP == 0
    n_heads = DQ // dl
    d_att = wuv_pad.shape[2]
    topk = min(TOPK_MAX, S // 4)
    QB, KC = Q_BLOCK, KEY_CHUNK
    nkb = S // QB
    assert nkb % KC == 0
    return pl.pallas_call(
        functools.partial(_dsa_kernel, topk, n_heads),
        grid=(B, nkb),
        in_specs=[pl.BlockSpec((1, QB, DQ), lambda b, i: (b, i, 0)),
                  pl.BlockSpec((1, S, dl), lambda b, i: (b, 0, 0)),
                  pl.BlockSpec((1, QB, iq.shape[2]), lambda b, i: (b, i, 0)),
                  pl.BlockSpec((1, S, LANES), lambda b, i: (b, 0, 0)),
                  pl.BlockSpec((1, QB, LANES), lambda b, i: (b, i, 0)),
                  _const_spec(bias_near.shape), _const_spec(wuv_pad.shape)],
        out_specs=pl.BlockSpec((1, QB, d_att), lambda b, i: (b, i, 0)),
        out_shape=jax.ShapeDtypeStruct((B, S, d_att), BF16),
        scratch_shapes=[pltpu.VMEM(((nkb + KC - 1) * QB, dl), BF16),
                        pltpu.VMEM((nkb, QB, QB), I32), pltpu.VMEM((N_DIGITS, nkb, QB, QB), DIGIT),
                        pltpu.VMEM((nkb + KC - 1, QB, QB), F32),
                        pltpu.VMEM((n_heads, QB, QB), F32), pltpu.VMEM((n_heads, QB, QB), F32),
                        pltpu.VMEM((n_heads * QB, dl), F32), pltpu.VMEM((1, QB), I32),
                        pltpu.VMEM((1, QB), F32)],
        compiler_params=_cparams("parallel", "arbitrary"),
        name="dsa",
    )(aq, c, iq, ikw, ikw, bias_near, wuv_pad)


def _t5_bucket(n):
    nf = jnp.maximum(n, 1).astype(F32)
    large = MAX_EXACT + (jnp.log(nf / MAX_EXACT) / math.log(MAX_DISTANCE / MAX_EXACT)
                         * (N_BUCKETS - MAX_EXACT)).astype(I32)
    large = jnp.minimum(large, N_BUCKETS - 1)
    return jnp.where(n < MAX_EXACT, n, large)


def _near_bias(rel_bias):
    QB = Q_BLOCK
    assert QB >= MAX_DISTANCE
    r = jnp.arange(QB)[:, None] - jnp.arange(QB)[None, :]
    def lookup(bucket):
        one_hot = jax.nn.one_hot(bucket, N_BUCKETS, dtype=F32)
        return jnp.einsum('rcb,bh->rch', one_hot, rel_bias.astype(F32), precision=lax.Precision.HIGHEST)

    far = rel_bias[N_BUCKETS - 1]
    diag = lookup(_t5_bucket(jnp.maximum(r, 0))) - far
    prev = lookup(_t5_bucket(r + QB)) - far
    return (jnp.stack([diag, prev]).transpose(0, 3, 1, 2) * LOG2E).astype(F32)


def _merge_kernel(alpha, D, h_ref, u_ref, r_ref, a_ref, gt_ref, wc_ref, wr_ref, wa_ref, wo_ref, g_ref, b_ref,
                  o_ref):
    y = gt_ref[:, 0:D].astype(F32) * _dot(u_ref[...], wc_ref[...])
    y = y + gt_ref[:, D:2 * D].astype(F32) * _dot(r_ref[...], wr_ref[...])
    y = y + gt_ref[:, 2 * D:3 * D].astype(F32) * _dot(a_ref[...], wa_ref[...])
    z = alpha * h_ref[...] + _dot(y.astype(BF16), wo_ref[...])
    o_ref[...] = _ln_rows(z, g_ref[...], b_ref[...])


def _merge(h, u, r, a, gt, wc, wr, wa, wo, g, b, alpha, tm):
    T, D = h.shape
    row = lambda n: pl.BlockSpec((tm, n), lambda i: (i, 0))
    return pl.pallas_call(
        functools.partial(_merge_kernel, alpha, D),
        grid=(T // tm,),
        in_specs=[row(D), row(u.shape[1]), row(r.shape[1]), row(a.shape[1]), row(3 * D),
                  _const_spec(wc.shape), _const_spec(wr.shape), _const_spec(wa.shape), _const_spec(wo.shape),
                  _const_spec((1, D)), _const_spec((1, D))],
        out_specs=row(D),
        out_shape=jax.ShapeDtypeStruct((T, D), F32),
        compiler_params=_cparams("parallel"),
        name="merge",
    )(h, u, r, a, gt, wc, wr, wa, wo, g.reshape(1, D), b.reshape(1, D))


def _split_bf16(x):
    hi = x.astype(BF16)
    return hi, (x - hi.astype(F32)).astype(BF16)


def _moe_kernel(alpha, n_exp, de, h_ref, rw_hi_ref, rw_lo_ref, rb_ref, wgu_ref, wdn_ref, wsgu_ref, wsdn_ref,
                g_ref, b_ref, o_ref, xb_ref, gate_ref, acc_ref):
    e = pl.program_id(1)
    lane = lax.broadcasted_iota(I32, gate_ref.shape, 1)

    @pl.when(e == 0)
    def _():
        x = h_ref[...]
        x_hi, x_lo = _split_bf16(x)
        xb_ref[...] = x_hi
        logits = _dot(x_hi, rw_hi_ref[...]) + (_dot(x_lo, rw_hi_ref[...]) + _dot(x_hi, rw_lo_ref[...]))
        scores = _sigmoid(logits)
        sb = scores + rb_ref[...]
        sel = jnp.zeros(sb.shape, jnp.bool_)
        for _ in range(TOPK_EXPERTS):
            m = jnp.max(sb, axis=-1, keepdims=True)
            first = jnp.min(jnp.where(sb == m, lane, n_exp), axis=-1, keepdims=True)
            hit = lane == first
            sel = sel | hit
            sb = jnp.where(hit, -jnp.inf, sb)
        w = jnp.where(sel, scores, 0.0)
        gate_ref[...] = w / jnp.sum(w, axis=-1, keepdims=True) * ROUTE_SCALE
        ds = wsdn_ref.shape[0]
        s = _dot(x_hi, wsgu_ref[...])
        a_s, u_s = s[:, :ds], s[:, ds:]
        acc_ref[...] = _dot((a_s * _sigmoid(a_s) * u_s).astype(BF16), wsdn_ref[...])

    xb = xb_ref[...]
    n_grp = wgu_ref.shape[0]
    hmids = []
    for i in range(n_grp):
        au = _dot(xb, wgu_ref[i])
        a, u = au[:, :de], au[:, de:]
        ge = jnp.sum(jnp.where(lane == e * n_grp + i, gate_ref[...], 0.0), axis=-1, keepdims=True)
        hmids.append((a * _sigmoid(a) * u * ge).astype(BF16))
    acc_ref[...] += _dot(jnp.concatenate(hmids, axis=1), wdn_ref[...].reshape(n_grp * de, -1))

    @pl.when(e == n_exp // n_grp - 1)
    def _():
        o_ref[...] = _ln_rows(alpha * h_ref[...] + acc_ref[...], g_ref[...], b_ref[...])


def _moe(h, rw, rb, wgu, wdn, wsgu, wsdn, g, b, alpha, tm):
    T, D = h.shape
    n_exp, _, de2 = wgu.shape
    de = de2 // 2
    rw_hi, rw_lo = _split_bf16(rw)
    return pl.pallas_call(
        functools.partial(_moe_kernel, alpha, n_exp, de),
        grid=(T // tm, n_exp // MOE_EXPERT_GROUP),
        in_specs=[pl.BlockSpec((tm, D), lambda i, e: (i, 0)),
                  _const_spec(rw.shape), _const_spec(rw.shape), _const_spec((1, n_exp)),
                  pl.BlockSpec((MOE_EXPERT_GROUP, D, de2), lambda i, e: (e, 0, 0)),
                  pl.BlockSpec((MOE_EXPERT_GROUP, de, D), lambda i, e: (e, 0, 0)),
                  _const_spec(wsgu.shape), _const_spec(wsdn.shape),
                  _const_spec((1, D)), _const_spec((1, D))],
        out_specs=pl.BlockSpec((tm, D), lambda i, e: (i, 0)),
        out_shape=jax.ShapeDtypeStruct((T, D), F32),
        scratch_shapes=[pltpu.VMEM((tm, D), BF16), pltpu.VMEM((tm, n_exp), F32), pltpu.VMEM((tm, D), F32)],
        compiler_params=_cparams("parallel", "arbitrary"),
        name="moe",
    )(h, rw_hi, rw_lo, rb.reshape(1, n_exp), wgu, wdn, wsgu, wsdn, g.reshape(1, D), b.reshape(1, D))


def _pack_in_weights(w, b, dims):
    dc, dr, daq, dl, diq, dg = dims
    aligned = [2 * dc, dr, dr, dr, dr, daq, dl, diq]
    assert all(s % LANES == 0 for s in aligned) and IDX_HEAD_DIM + N_IDX_HEADS <= LANES
    split = sum(aligned) + IDX_HEAD_DIM + N_IDX_HEADS
    assert split + dg == w.shape[1]
    pad = LANES - IDX_HEAD_DIM - N_IDX_HEADS
    wb = w.astype(BF16)
    w_pack = jnp.concatenate([wb[:, :split], jnp.zeros((w.shape[0], pad), BF16), wb[:, split:]], axis=1)
    b_pack = jnp.concatenate([b[:split], jnp.zeros((pad,), b.dtype), b[split:]])
    return w_pack, b_pack.reshape(1, -1).astype(F32)


def kernel(x, ln_in_g, ln_in_b, w_in, b_in, conv_dw, conv_b, conv_ln_g, conv_ln_b, w_conv_proj, hgrn_gamma,
           hgrn_norm_g, w_rnn_proj, kv_norm_g, w_uv, w_att_proj, rel_bias, w_out, ln_mix_g, ln_mix_b,
           router_w, router_b, w_expert_gu, w_expert_dn, w_shared_gu, w_shared_dn, ln_ffn_g, ln_ffn_b):
    B, S, D = x.shape
    T = B * S
    depth = w_in.shape[0]
    dc = conv_dw.shape[2]
    dr = hgrn_gamma.shape[1]
    n_heads, dl, dv = w_uv.shape[1:]
    daq = n_heads * dl
    diq = N_IDX_HEADS * IDX_HEAD_DIM
    dims = (dc, dr, daq, dl, diq, 3 * D)
    alpha = (2 * depth) ** 0.25

    rows = {name: min(n, S if name in ("conv", "hgrn2") else T) for name, n in TILE_ROWS.items()}
    lb_all = jnp.cumsum(jax.nn.softmax(hgrn_gamma.astype(F32), axis=0), axis=0)
    lb_all = lb_all - lb_all[0]
    bias_near = _near_bias(rel_bias)

    h = _layer_norm(x.reshape(T, D), ln_in_g, ln_in_b, rows["ln"])
    for l in range(depth):
        w_pack, b_pack = _pack_in_weights(w_in[l], b_in[l], dims)
        u, rq, rf, ri, rg, aq, c, iq, ikw, gt = _in_projection(h, w_pack, b_pack, kv_norm_g[l], dims,
                                                                rows["in_proj"])

        def seq(t):
            return t.reshape(B, S, t.shape[-1])

        y_conv = _conformer_conv(seq(u), conv_dw[l], conv_b[l], conv_ln_g[l], conv_ln_b[l], rows["conv"])
        y_rnn = _hgrn2(seq(rq), seq(rf), seq(ri), seq(rg), lb_all[l], hgrn_norm_g[l], rows["hgrn2"])
        wuv_pad = jnp.zeros((n_heads, dl, n_heads * dv), F32)
        for hh in range(n_heads):
            wuv_pad = wuv_pad.at[hh, :, hh * dv:(hh + 1) * dv].set(w_uv[l, hh])
        y_att = _dsa_attention(seq(aq), seq(c), seq(iq), seq(ikw), bias_near, wuv_pad.astype(BF16))
        h = _merge(h, y_conv.reshape(T, -1), y_rnn.reshape(T, -1), y_att.reshape(T, -1), gt,
                   w_conv_proj[l].astype(BF16), w_rnn_proj[l].astype(BF16), w_att_proj[l].astype(BF16),
                   w_out[l].astype(BF16), ln_mix_g[l], ln_mix_b[l], alpha, rows["merge"])
        h = _moe(h, router_w[l], router_b[l], w_expert_gu[l].astype(BF16), w_expert_dn[l].astype(BF16),
                 w_shared_gu[l].astype(BF16), w_shared_dn[l].astype(BF16), ln_ffn_g[l], ln_ffn_b[l], alpha,
                 rows["moe"])
    return h.reshape(B, S, D)
```
